```python
import jax, jax.numpy as jnp
from jax import lax
import numpy as np

D_MODEL = 2048
BATCH = 8
SEQ = 4096
DEPTH = 4

N_MIXERS = 2
N_A = (DEPTH + 1) // 2
N_B = DEPTH // 2
HEAD_DIM = 128
N_HEADS = D_MODEL // HEAD_DIM
Q_BLOCK = 128
CHUNK = 128
D_GM = D_MODEL
GM_GROUP = 128
N_GM_GROUPS = D_GM // GM_GROUP
D_FF = 5504
CONV_W = 3
EPS = 1e-6

kernel_name = "hybrid_fox_gmlp_convffn_adaln"


def rms_norm(x, g):
    xf = x.astype(jnp.float32)
    y = xf * lax.rsqrt(jnp.mean(xf * xf, axis=-1, keepdims=True) + EPS)
    return (y * g.astype(jnp.float32)).astype(x.dtype)


def modulate(h, shift, scale):
    return h * (1 + scale[:, None, :]) + shift[:, None, :]


def fox_attention(h, w_in, b_f, w_o):
    B, S, D = h.shape
    proj = h @ w_in
    q = proj[..., :D].reshape(B, S, N_HEADS, HEAD_DIM).transpose(0, 2, 1, 3)
    k = proj[..., D:2 * D].reshape(B, S, N_HEADS, HEAD_DIM).transpose(0, 2, 1, 3)
    v = proj[..., 2 * D:3 * D].reshape(B, S, N_HEADS, HEAD_DIM).transpose(0, 2, 1, 3)
    f_logit = (proj[..., 3 * D:] + b_f).astype(jnp.float32)
    log_f = jax.nn.log_sigmoid(f_logit)
    F = jnp.cumsum(log_f, axis=1).transpose(0, 2, 1)
    scale = HEAD_DIM ** -0.5
    outs = []
    for i in range(S // Q_BLOCK):
        lo, hi = i * Q_BLOCK, (i + 1) * Q_BLOCK
        qb = q[:, :, lo:hi]
        kb = k[:, :, :hi]
        vb = v[:, :, :hi]
        s = jnp.einsum('bhqd,bhkd->bhqk', qb, kb).astype(jnp.float32) * scale
        s = s + F[:, :, lo:hi, None] - F[:, :, None, :hi]
        q_pos = lo + jnp.arange(Q_BLOCK)
        k_pos = jnp.arange(hi)
        mask = k_pos[None, :] <= q_pos[:, None]
        s = jnp.where(mask, s, -jnp.inf)
        p = jax.nn.softmax(s, axis=-1).astype(v.dtype)
        outs.append(jnp.einsum('bhqk,bhkd->bhqd', p, vb))
    o = jnp.concatenate(outs, axis=2)
    o = o.transpose(0, 2, 1, 3).reshape(B, S, D)
    return o @ w_o


def chunked_gmlp(h, w_in, v_g, w_s, b_s, w_o):
    B, S, _ = h.shape
    z = jax.nn.gelu(h @ w_in)
    u, v = z[..., :D_GM], z[..., D_GM:]
    v = rms_norm(v, v_g)
    vc = v.reshape(B, S // CHUNK, CHUNK, N_GM_GROUPS, GM_GROUP)
    causal = jnp.tril(jnp.ones((CHUNK, CHUNK), dtype=w_s.dtype))
    w = w_s * causal[None]
    sv = jnp.einsum('gts,bnsgd->bntgd', w, vc)
    sv = sv + b_s.T[None, None, :, :, None]
    gated = u * sv.reshape(B, S, D_GM)
    return gated @ w_o


def conv_ffn(h, w_in, conv_w, conv_b, w_out):
    a = h @ w_in
    S = a.shape[1]
    ap = jnp.pad(a, ((0, 0), (CONV_W - 1, 0), (0, 0)))
    a = (conv_w[0] * ap[:, 0:S] + conv_w[1] * ap[:, 1:S + 1]
         + conv_w[2] * ap[:, 2:S + 2] + conv_b)
    gate, up = a[..., :D_FF], a[..., D_FF:]
    return (jax.nn.silu(gate) * up) @ w_out


def setup_inputs(seed: int = 0) -> dict:
    key = jax.random.key(seed)
    ks = jax.random.split(key, 24)
    f32 = jnp.float32
    nrm = lambda k, shape, s: jax.random.normal(k, shape, f32) * s
    D = D_MODEL
    return {
        "x": nrm(ks[0], (BATCH, SEQ, D), 1.0),
        "c": nrm(ks[1], (BATCH, D), 1.0),
        "mod_w": nrm(ks[2], (DEPTH, D, 6 * D), 0.5 * D ** -0.5),
        "mod_b": nrm(ks[3], (DEPTH, 6 * D), 0.02),
        "mix_norm_g": 1.0 + nrm(ks[4], (DEPTH, D), 0.02),
        "ffn_norm_g": 1.0 + nrm(ks[5], (DEPTH, D), 0.02),
        "attn_w_in": nrm(ks[6], (N_A, D, 3 * D + N_HEADS), D ** -0.5),
        "attn_b_f": jax.random.uniform(ks[7], (N_A, N_HEADS), f32, 1.0, 6.0),
        "attn_w_o": nrm(ks[8], (N_A, D, D), D ** -0.5),
        "gm_w_in": nrm(ks[9], (N_B, D, 2 * D_GM), D ** -0.5),
        "gm_v_g": 1.0 + nrm(ks[10], (N_B, D_GM), 0.02),
        "gm_w_s": nrm(ks[11], (N_B, N_GM_GROUPS, CHUNK, CHUNK), CHUNK ** -0.5),
        "gm_b_s": 1.0 + nrm(ks[12], (N_B, N_GM_GROUPS, CHUNK), 0.1),
        "gm_w_o": nrm(ks[13], (N_B, D_GM, D), D_GM ** -0.5),
        "ffn_w_in": nrm(ks[14], (DEPTH, D, 2 * D_FF), D ** -0.5),
        "ffn_conv_w": nrm(ks[15], (DEPTH, CONV_W, 2 * D_FF), CONV_W ** -0.5),
        "ffn_conv_b": nrm(ks[16], (DEPTH, 2 * D_FF), 0.02),
        "ffn_w_out": nrm(ks[17], (DEPTH, D_FF, D), D_FF ** -0.5),
        "final_g": 1.0 + nrm(ks[18], (D,), 0.02),
    }


def reference(x, c, mod_w, mod_b, mix_norm_g, ffn_norm_g, attn_w_in, attn_b_f, attn_w_o,
              gm_w_in, gm_v_g, gm_w_s, gm_b_s, gm_w_o, ffn_w_in, ffn_conv_w, ffn_conv_b,
              ffn_w_out, final_g):
    c_act = jax.nn.silu(c)
    for i in range(DEPTH):
        mod = c_act @ mod_w[i] + mod_b[i]
        sh1, sc1, g1, sh2, sc2, g2 = jnp.split(mod, 6, axis=-1)
        h = modulate(rms_norm(x, mix_norm_g[i]), sh1, sc1)
        j = i // N_MIXERS
        if i % N_MIXERS == 0:
            y = fox_attention(h, attn_w_in[j], attn_b_f[j], attn_w_o[j])
        else:
            y = chunked_gmlp(h, gm_w_in[j], gm_v_g[j], gm_w_s[j], gm_b_s[j], gm_w_o[j])
        x = x + g1[:, None, :] * y
        h = modulate(rms_norm(x, ffn_norm_g[i]), sh2, sc2)
        x = x + g2[:, None, :] * conv_ffn(h, ffn_w_in[i], ffn_conv_w[i], ffn_conv_b[i], ffn_w_out[i])
    return rms_norm(x, final_g)
```

```python
import functools

import jax
import jax.numpy as jnp
from jax import lax
from jax.experimental import pallas as pl
from jax.experimental.pallas import tpu as pltpu

HEAD_DIM = 128
CHUNK = 128
GM_GROUP = 128
CONV_W = 3
EPS = 1e-6
LANES = 128
SUBLANES = 8
VMEM_LIMIT_BYTES = 56 * 1024 * 1024

F32 = jnp.float32
BF16 = jnp.bfloat16


def _pick(n, prefs):
    for p in prefs:
        if n % p == 0:
            return p
    return n


def _round_up(n, m):
    return (n + m - 1) // m * m


def _params(*sem):
    return pltpu.CompilerParams(dimension_semantics=sem, vmem_limit_bytes=VMEM_LIMIT_BYTES)


def _silu(x):
    return x / (1.0 + jnp.exp(-x))


def _mod_kernel(c_ref, w_ref, b_ref, o_ref):
    ca = _silu(c_ref[...]).astype(BF16)
    o_ref[0] = jnp.dot(ca, w_ref[0].astype(BF16), preferred_element_type=F32) + b_ref[0]


def _modulation(c, mod_w, mod_b):
    depth, d, n = mod_w.shape
    b = c.shape[0]
    tn = _pick(n, (1024, 512, 256, 128))
    return pl.pallas_call(
        _mod_kernel,
        grid=(depth, n // tn),
        in_specs=[
            pl.BlockSpec((b, d), lambda l, j: (0, 0)),
            pl.BlockSpec((1, d, tn), lambda l, j: (l, 0, j)),
            pl.BlockSpec((1, 1, tn), lambda l, j: (l, 0, j)),
        ],
        out_specs=pl.BlockSpec((1, b, tn), lambda l, j: (l, 0, j)),
        out_shape=jax.ShapeDtypeStruct((depth, b, n), F32),
        compiler_params=_params("arbitrary", "arbitrary"),
        name="adaln_modulation",
    )(c, mod_w, mod_b.reshape(depth, 1, n))


def _norm_modulate(x_ref, g_ref, sc_ref, sh_ref, h_ref):
    x = x_ref[...]
    r = lax.rsqrt(jnp.mean(x * x, axis=-1, keepdims=True) + EPS)
    a = g_ref[...] * (1.0 + sc_ref[0])
    h_ref[...] = ((x * r) * a + sh_ref[0]).astype(BF16)


def _row_specs(tm, d, rows_per_batch):
    tiles_per_batch = rows_per_batch // tm
    return [
        pl.BlockSpec((tm, d), lambda i, j: (i, 0)),
        pl.BlockSpec((1, d), lambda i, j: (0, 0)),
        pl.BlockSpec((1, 1, d), lambda i, j: (i // tiles_per_batch, 0, 0)),
        pl.BlockSpec((1, 1, d), lambda i, j: (i // tiles_per_batch, 0, 0)),
    ]


def _attn_in_kernel(x_ref, g_ref, sc_ref, sh_ref, w_ref, wf_ref, bf_ref, qkv_ref, fl_ref, h_ref,
                    *, q_tiles, q_scale):
    j = pl.program_id(1)

    @pl.when(j == 0)
    def _():
        _norm_modulate(x_ref, g_ref, sc_ref, sh_ref, h_ref)
        fl_ref[...] = jnp.dot(h_ref[...], wf_ref[...], preferred_element_type=F32) + bf_ref[...]

    acc = jnp.dot(h_ref[...], w_ref[...], preferred_element_type=F32)
    acc = acc * jnp.where(j < q_tiles, q_scale, 1.0)
    qkv_ref[...] = acc.astype(BF16)


def _attn_in_proj(x2, g, sc, sh, w_qkv, w_f, b_f, rows_per_batch):
    m, d = x2.shape
    n = w_qkv.shape[1]
    tm = _pick(rows_per_batch, (1024, 512, 256, 128))
    tn = _pick(d, (1024, 512, 256, 128))
    kern = functools.partial(_attn_in_kernel, q_tiles=d // tn, q_scale=HEAD_DIM ** -0.5)
    return pl.pallas_call(
        kern,
        grid=(m // tm, n // tn),
        in_specs=_row_specs(tm, d, rows_per_batch) + [
            pl.BlockSpec((d, tn), lambda i, j: (0, j)),
            pl.BlockSpec((d, LANES), lambda i, j: (0, 0)),
            pl.BlockSpec((1, LANES), lambda i, j: (0, 0)),
        ],
        out_specs=[
            pl.BlockSpec((tm, tn), lambda i, j: (i, j)),
            pl.BlockSpec((tm, LANES), lambda i, j: (i, 0)),
        ],
        out_shape=[
            jax.ShapeDtypeStruct((m, n), BF16),
            jax.ShapeDtypeStruct((m, LANES), F32),
        ],
        scratch_shapes=[pltpu.VMEM((tm, d), BF16)],
        compiler_params=_params("arbitrary", "arbitrary"),
        name="attn_in_proj",
    )(x2, g, sc, sh, w_qkv, w_f, b_f)


def _split3(x):
    hi = x.astype(BF16)
    r1 = x - hi.astype(F32)
    mid = r1.astype(BF16)
    lo = (r1 - mid.astype(F32)).astype(BF16)
    return hi, mid, lo


def _forget_cumsum_kernel(fl_ref, f_ref, ft_ref, carry_ref, *, sub):
    @pl.when(pl.program_id(1) == 0)
    def _():
        carry_ref[...] = jnp.zeros_like(carry_ref)

    tc = fl_ref.shape[0]
    row = lax.broadcasted_iota(jnp.int32, (sub, sub), 0)
    col = lax.broadcasted_iota(jnp.int32, (sub, sub), 1)
    tril = jnp.where(row >= col, 1.0, 0.0).astype(BF16)
    carry = carry_ref[0:1, :]
    for r in range(tc // sub):
        z = fl_ref[r * sub:(r + 1) * sub, :]
        lf = jnp.minimum(z, 0.0) - jnp.log1p(jnp.exp(-jnp.abs(z)))
        hi, mid, lo = _split3(lf)
        c = (jnp.dot(tril, lo, preferred_element_type=F32)
             + jnp.dot(tril, mid, preferred_element_type=F32)
             + jnp.dot(tril, hi, preferred_element_type=F32)) + carry
        f_ref[r * sub:(r + 1) * sub, :] = c
        ft_ref[0, :, r * sub:(r + 1) * sub] = c.T
        carry = c[sub - 1:sub, :]
    carry_ref[0:1, :] = carry


def _forget_cumsum(flog, batch, seq):
    tc = _pick(seq, (1024, 512, 256, 128))
    sub = _pick(tc, (256, 128))
    nt = seq // tc
    return pl.pallas_call(
        functools.partial(_forget_cumsum_kernel, sub=sub),
        grid=(batch, nt),
        in_specs=[pl.BlockSpec((tc, LANES), lambda b, t: (b * nt + t, 0))],
        out_specs=[
            pl.BlockSpec((tc, LANES), lambda b, t: (b * nt + t, 0)),
            pl.BlockSpec((1, LANES, tc), lambda b, t: (b, 0, t)),
        ],
        out_shape=[
            jax.ShapeDtypeStruct((batch * seq, LANES), F32),
            jax.ShapeDtypeStruct((batch, LANES, seq), F32),
        ],
        scratch_shapes=[pltpu.VMEM((SUBLANES, LANES), F32)],
        compiler_params=_params("arbitrary", "arbitrary"),
        name="forget_cumsum",
    )(flog)


def _fox_attn_kernel(q_ref, k_ref, v_ref, f_ref, ft_ref, o_ref, *, t):
    h = pl.program_id(1)
    qi = pl.program_id(2)
    q = q_ref[...]
    lane = lax.broadcasted_iota(jnp.int32, (t, LANES), 1)
    fq = jnp.sum(jnp.where(lane == h, f_ref[...], 0.0), axis=-1, keepdims=True)

    def step(j, carry, masked):
        m_prev, l_prev, acc_prev = carry
        start = pl.multiple_of(j * t, t)
        kj = k_ref[pl.ds(start, t), :]
        vj = v_ref[pl.ds(start, t), :]
        fk = ft_ref[0, h, pl.ds(j, 1), :]
        s = lax.dot_general(q, kj, (((1,), (1,)), ((), ())), preferred_element_type=F32)
        s = s + fq - fk
        if masked:
            row = lax.broadcasted_iota(jnp.int32, (t, t), 0)
            col = lax.broadcasted_iota(jnp.int32, (t, t), 1)
            s = jnp.where(col <= row, s, -jnp.inf)
        m_new = jnp.maximum(m_prev, jnp.max(s, axis=-1, keepdims=True))
        alpha = jnp.exp(m_prev - m_new)
        p = jnp.exp(s - m_new)
        l_new = alpha * l_prev + jnp.sum(p, axis=-1, keepdims=True)
        acc_new = alpha * acc_prev + jnp.dot(p.astype(BF16), vj, preferred_element_type=F32)
        return m_new, l_new, acc_new

    init = (jnp.full((t, 1), -1e30, F32), jnp.zeros((t, 1), F32), jnp.zeros((t, HEAD_DIM), F32))
    carry = lax.fori_loop(0, qi, functools.partial(step, masked=False), init)
    _, l_fin, acc = step(qi, carry, True)
    o_ref[...] = (acc / l_fin).astype(BF16)


def _fox_attention(qkv, f, ft4, batch, seq, d):
    heads = d // HEAD_DIM
    t = ft4.shape[-1]
    nq = seq // t
    return pl.pallas_call(
        functools.partial(_fox_attn_kernel, t=t),
        grid=(batch, heads, nq),
        in_specs=[
            pl.BlockSpec((t, HEAD_DIM), lambda b, h, i: (b * nq + i, h)),
            pl.BlockSpec((seq, HEAD_DIM), lambda b, h, i: (b, heads + h)),
            pl.BlockSpec((seq, HEAD_DIM), lambda b, h, i: (b, 2 * heads + h)),
            pl.BlockSpec((t, LANES), lambda b, h, i: (b * nq + i, 0)),
            pl.BlockSpec((1, heads, nq, t), lambda b, h, i: (b, 0, 0, 0)),
        ],
        out_specs=pl.BlockSpec((t, HEAD_DIM), lambda b, h, i: (b * nq + i, h)),
        out_shape=jax.ShapeDtypeStruct((batch * seq, d), BF16),
        compiler_params=_params("arbitrary", "arbitrary", "arbitrary"),
        name="fox_attention",
    )(qkv, qkv, qkv, f, ft4)


def _gelu_tanh(x):
    c = 0.7978845608028654
    return 0.5 * x * (1.0 + jnp.tanh(c * (x + 0.044715 * (x * x * x))))


def _gm_in_kernel(x_ref, g_ref, sc_ref, sh_ref, w_ref, z_ref, h_ref):
    @pl.when(pl.program_id(1) == 0)
    def _():
        _norm_modulate(x_ref, g_ref, sc_ref, sh_ref, h_ref)

    acc = jnp.dot(h_ref[...], w_ref[...], preferred_element_type=F32)
    z_ref[...] = _gelu_tanh(acc).astype(BF16)


def _gm_in_proj(x2, g, sc, sh, w, rows_per_batch):
    m, d = x2.shape
    n = w.shape[1]
    tm = _pick(rows_per_batch, (1024, 512, 256, 128))
    tn = _pick(n, (1024, 512, 256, 128))
    return pl.pallas_call(
        _gm_in_kernel,
        grid=(m // tm, n // tn),
        in_specs=_row_specs(tm, d, rows_per_batch) + [pl.BlockSpec((d, tn), lambda i, j: (0, j))],
        out_specs=pl.BlockSpec((tm, tn), lambda i, j: (i, j)),
        out_shape=jax.ShapeDtypeStruct((m, n), BF16),
        scratch_shapes=[pltpu.VMEM((tm, d), BF16)],
        compiler_params=_params("arbitrary", "arbitrary"),
        name="gmlp_in_proj",
    )(x2, g, sc, sh, w)


def _gm_gate_kernel(u_ref, v_ref, vg_ref, ws_ref, bst_ref, o_ref):
    rows, dg = v_ref.shape
    groups = dg // GM_GROUP
    v = v_ref[...].astype(F32)
    r = lax.rsqrt(jnp.mean(v * v, axis=-1, keepdims=True) + EPS)
    vn = ((v * r) * vg_ref[...]).astype(BF16)
    row = lax.broadcasted_iota(jnp.int32, (CHUNK, CHUNK), 0)
    col = lax.broadcasted_iota(jnp.int32, (CHUNK, CHUNK), 1)
    causal = col <= row
    for g in range(groups):
        w = jnp.where(causal, ws_ref[g], 0.0).astype(BF16)
        bias = bst_ref[:, g:g + 1]
        lo = g * GM_GROUP
        for c in range(rows // CHUNK):
            r0 = c * CHUNK
            sv = jnp.dot(w, vn[r0:r0 + CHUNK, lo:lo + GM_GROUP], preferred_element_type=F32) + bias
            u = u_ref[r0:r0 + CHUNK, lo:lo + GM_GROUP].astype(F32)
            o_ref[r0:r0 + CHUNK, lo:lo + GM_GROUP] = (u * sv).astype(BF16)


def _gm_gate(z, v_g, w_s, b_s_t, rows_per_batch):
    m, n2 = z.shape
    dg = n2 // 2
    groups = dg // GM_GROUP
    tr = _pick(rows_per_batch, (512, 256, 128))
    return pl.pallas_call(
        _gm_gate_kernel,
        grid=(m // tr,),
        in_specs=[
            pl.BlockSpec((tr, dg), lambda i: (i, 0)),
            pl.BlockSpec((tr, dg), lambda i: (i, 1)),
            pl.BlockSpec((1, dg), lambda i: (0, 0)),
            pl.BlockSpec((groups, CHUNK, CHUNK), lambda i: (0, 0, 0)),
            pl.BlockSpec((CHUNK, groups), lambda i: (0, 0)),
        ],
        out_specs=pl.BlockSpec((tr, dg), lambda i: (i, 0)),
        out_shape=jax.ShapeDtypeStruct((m, dg), BF16),
        compiler_params=_params("arbitrary"),
        name="gmlp_spatial_gate",
    )(z, z, v_g, w_s, b_s_t)


def _ffn_in_kernel(x_ref, g_ref, sc_ref, sh_ref, wg_ref, wu_ref, cwg_ref, cwu_ref, cbg_ref, cbu_ref,
                   o_ref, h_ref, sg_ref, su_ref, carry_g_ref, carry_u_ref, *, tiles_per_batch):
    i = pl.program_id(0)
    j = pl.program_id(1)
    tm = x_ref.shape[0]
    halo = SUBLANES

    @pl.when(j == 0)
    def _():
        _norm_modulate(x_ref, g_ref, sc_ref, sh_ref, h_ref)

    first_in_batch = (i % tiles_per_batch) == 0

    def conv(acc, s_ref, carry_ref, cw_ref, cb_ref):
        @pl.when(first_in_batch)
        def _():
            s_ref[0:halo, :] = jnp.zeros((halo, s_ref.shape[1]), F32)

        @pl.when(jnp.logical_not(first_in_batch))
        def _():
            s_ref[0:halo, :] = carry_ref[j]

        s_ref[halo:halo + tm, :] = acc
        carry_ref[j] = acc[tm - halo:tm, :]
        a1 = s_ref[halo - 1:halo - 1 + tm, :]
        a2 = s_ref[halo - 2:halo - 2 + tm, :]
        return cw_ref[0:1, :] * a2 + cw_ref[1:2, :] * a1 + cw_ref[2:3, :] * acc + cb_ref[...]

    h = h_ref[...]
    gate = conv(jnp.dot(h, wg_ref[...], preferred_element_type=F32), sg_ref, carry_g_ref, cwg_ref, cbg_ref)
    up = conv(jnp.dot(h, wu_ref[...], preferred_element_type=F32), su_ref, carry_u_ref, cwu_ref, cbu_ref)
    o_ref[...] = (_silu(gate) * up).astype(BF16)


def _ffn_in(x2, g, sc, sh, wg, wu, cwg, cwu, cbg, cbu, rows_per_batch):
    m, d = x2.shape
    fp = wg.shape[1]
    tm = _pick(rows_per_batch, (1024, 512, 256, 128))
    tn = _pick(fp, (512, 256, 128))
    nj = fp // tn
    kern = functools.partial(_ffn_in_kernel, tiles_per_batch=rows_per_batch // tm)
    col = lambda i, j: (0, j)
    return pl.pallas_call(
        kern,
        grid=(m // tm, nj),
        in_specs=_row_specs(tm, d, rows_per_batch) + [
            pl.BlockSpec((d, tn), col),
            pl.BlockSpec((d, tn), col),
            pl.BlockSpec((CONV_W, tn), col),
            pl.BlockSpec((CONV_W, tn), col),
            pl.BlockSpec((1, tn), col),
            pl.BlockSpec((1, tn), col),
        ],
        out_specs=pl.BlockSpec((tm, tn), lambda i, j: (i, j)),
        out_shape=jax.ShapeDtypeStruct((m, fp), BF16),
        scratch_shapes=[
            pltpu.VMEM((tm, d), BF16),
            pltpu.VMEM((tm + SUBLANES, tn), F32),
            pltpu.VMEM((tm + SUBLANES, tn), F32),
            pltpu.VMEM((nj, SUBLANES, tn), F32),
            pltpu.VMEM((nj, SUBLANES, tn), F32),
        ],
        compiler_params=_params("arbitrary", "arbitrary"),
        name="ffn_in_conv_gate",
    )(x2, g, sc, sh, wg, wu, cwg, cwu, cbg, cbu)


def _out_proj_kernel(a_ref, w_ref, x_ref, gate_ref, o_ref):
    y = jnp.dot(a_ref[...], w_ref[...], preferred_element_type=F32)
    o_ref[...] = x_ref[...] + gate_ref[0] * y


def _out_proj_residual(a, w, x2, gate, rows_per_batch):
    m, k = a.shape
    n = w.shape[1]
    tm = _pick(rows_per_batch, (1024, 512, 256, 128))
    tn = _pick(n, (1024, 512, 256, 128))
    if k > 2 * n:
        tn = _pick(n, (512, 256, 128))
    tiles_per_batch = rows_per_batch // tm
    return pl.pallas_call(
        _out_proj_kernel,
        grid=(m // tm, n // tn),
        in_specs=[
            pl.BlockSpec((tm, k), lambda i, j: (i, 0)),
            pl.BlockSpec((k, tn), lambda i, j: (0, j)),
            pl.BlockSpec((tm, tn), lambda i, j: (i, j)),
            pl.BlockSpec((1, 1, tn), lambda i, j: (i // tiles_per_batch, 0, j)),
        ],
        out_specs=pl.BlockSpec((tm, tn), lambda i, j: (i, j)),
        out_shape=jax.ShapeDtypeStruct((m, n), F32),
        compiler_params=_params("arbitrary", "arbitrary"),
        name="out_proj_residual",
    )(a, w, x2, gate)


def _final_norm_kernel(x_ref, g_ref, o_ref):
    x = x_ref[...]
    r = lax.rsqrt(jnp.mean(x * x, axis=-1, keepdims=True) + EPS)
    o_ref[...] = (x * r) * g_ref[...]


def _final_norm(x2, g):
    m, d = x2.shape
    tm = _pick(m, (1024, 512, 256, 128))
    return pl.pallas_call(
        _final_norm_kernel,
        grid=(m // tm,),
        in_specs=[pl.BlockSpec((tm, d), lambda i: (i, 0)), pl.BlockSpec((1, d), lambda i: (0, 0))],
        out_specs=pl.BlockSpec((tm, d), lambda i: (i, 0)),
        out_shape=jax.ShapeDtypeStruct((m, d), F32),
        compiler_params=_params("arbitrary"),
        name="final_rms_norm",
    )(x2, g)


def _pad_cols(a, n):
    return jnp.pad(a, ((0, 0), (0, n - a.shape[1])))


def kernel(x, c, mod_w, mod_b, mix_norm_g, ffn_norm_g, attn_w_in, attn_b_f, attn_w_o, gm_w_in, gm_v_g, gm_w_s,
           gm_b_s, gm_w_o, ffn_w_in, ffn_conv_w, ffn_conv_b, ffn_w_out, final_g):
    batch, seq, d = x.shape
    depth = mod_w.shape[0]
    heads = d // HEAD_DIM
    d_ff = ffn_w_in.shape[-1] // 2
    fp = _round_up(d_ff, 512)
    m = batch * seq
    assert d % LANES == 0 and seq % CHUNK == 0 and heads <= LANES
    t_attn = _pick(seq, (512, 256, 128))

    x2 = x.reshape(m, d)
    mod = _modulation(c, mod_w, mod_b).reshape(depth, batch, 6, 1, d)

    for i in range(depth):
        sh1, sc1, g1, sh2, sc2, g2 = (mod[i, :, k] for k in range(6))
        j = i // 2
        gm = mix_norm_g[i].reshape(1, d)
        if i % 2 == 0:
            w_in = attn_w_in[j]
            w_qkv = w_in[:, :3 * d].astype(BF16)
            w_f = _pad_cols(w_in[:, 3 * d:], LANES).astype(BF16)
            b_f = _pad_cols(attn_b_f[j].reshape(1, heads), LANES)
            qkv, flog = _attn_in_proj(x2, gm, sc1, sh1, w_qkv, w_f, b_f, seq)
            f, ft = _forget_cumsum(flog, batch, seq)
            ft4 = ft[:, :heads, :].reshape(batch, heads, seq // t_attn, t_attn)
            y = _fox_attention(qkv, f, ft4, batch, seq, d)
            w_o = attn_w_o[j].astype(BF16)
        else:
            z = _gm_in_proj(x2, gm, sc1, sh1, gm_w_in[j].astype(BF16), seq)
            y = _gm_gate(z, gm_v_g[j].reshape(1, -1), gm_w_s[j], gm_b_s[j].T, seq)
            w_o = gm_w_o[j].astype(BF16)
        x2 = _out_proj_residual(y, w_o, x2, g1, seq)

        w_in = ffn_w_in[i]
        wg = _pad_cols(w_in[:, :d_ff], fp).astype(BF16)
        wu = _pad_cols(w_in[:, d_ff:], fp).astype(BF16)
        cw = ffn_conv_w[i]
        cb = ffn_conv_b[i].reshape(1, 2 * d_ff)
        a = _ffn_in(x2, ffn_norm_g[i].reshape(1, d), sc2, sh2, wg, wu,
                    _pad_cols(cw[:, :d_ff], fp), _pad_cols(cw[:, d_ff:], fp),
                    _pad_cols(cb[:, :d_ff], fp), _pad_cols(cb[:, d_ff:], fp), seq)
        w_out = jnp.pad(ffn_w_out[i], ((0, fp - d_ff), (0, 0))).astype(BF16)
        x2 = _out_proj_residual(a, w_out, x2, g2, seq)

    return _final_norm(x2, final_g.reshape(1, d)).reshape(batch, seq, d)
```

```python
import functools

import jax
import jax.numpy as jnp
from jax import lax
from jax.experimental import pallas as pl
from jax.experimental.pallas import tpu as pltpu

HEAD_DIM = 128
CHUNK = 128
GM_GROUP = 128
CONV_W = 3
EPS = 1e-6
LOG2E = 1.4426950408889634
LANES = 128
SUBLANES = 8
VMEM_LIMIT_BYTES = 56 * 1024 * 1024
FFN_ROW_CHUNK = 256

F32 = jnp.float32
BF16 = jnp.bfloat16


def _pick(n, prefs):
    for p in prefs:
        if n % p == 0:
            return p
    return n


def _round_up(n, m):
    return (n + m - 1) // m * m


def _params(*sem):
    return pltpu.CompilerParams(dimension_semantics=sem, vmem_limit_bytes=VMEM_LIMIT_BYTES)


def _silu(x):
    return x / (1.0 + jnp.exp(-x))


def _mod_kernel(c_ref, w_ref, b_ref, o_ref):
    ca = _silu(c_ref[...]).astype(BF16)
    o_ref[0] = jnp.dot(ca, w_ref[0].astype(BF16), preferred_element_type=F32) + b_ref[0]


def _modulation(c, mod_w, mod_b):
    depth, d, n = mod_w.shape
    b = c.shape[0]
    tn = _pick(n, (1024, 512, 256, 128))
    return pl.pallas_call(
        _mod_kernel,
        grid=(depth, n // tn),
        in_specs=[
            pl.BlockSpec((b, d), lambda l, j: (0, 0)),
            pl.BlockSpec((1, d, tn), lambda l, j: (l, 0, j)),
            pl.BlockSpec((1, 1, tn), lambda l, j: (l, 0, j)),
        ],
        out_specs=pl.BlockSpec((1, b, tn), lambda l, j: (l, 0, j)),
        out_shape=jax.ShapeDtypeStruct((depth, b, n), F32),
        compiler_params=_params("arbitrary", "arbitrary"),
        name="adaln_modulation",
    )(c, mod_w, mod_b.reshape(depth, 1, n))


def _norm_modulate(x_ref, g_ref, sc_ref, sh_ref, h_ref):
    x = x_ref[...]
    r = lax.rsqrt(jnp.mean(x * x, axis=-1, keepdims=True) + EPS)
    a = g_ref[...] * (1.0 + sc_ref[0])
    h_ref[...] = ((x * r) * a + sh_ref[0]).astype(BF16)


def _row_specs(tm, d, rows_per_batch):
    tiles_per_batch = rows_per_batch // tm
    return [
        pl.BlockSpec((tm, d), lambda i, j: (i, 0)),
        pl.BlockSpec((1, d), lambda i, j: (0, 0)),
        pl.BlockSpec((1, 1, d), lambda i, j: (i // tiles_per_batch, 0, 0)),
        pl.BlockSpec((1, 1, d), lambda i, j: (i // tiles_per_batch, 0, 0)),
    ]


def _attn_in_kernel(x_ref, g_ref, sc_ref, sh_ref, w_ref, wf_ref, bf_ref, qkv_ref, fl_ref, h_ref,
                    *, q_tiles, q_scale):
    j = pl.program_id(1)

    @pl.when(j == 0)
    def _():
        _norm_modulate(x_ref, g_ref, sc_ref, sh_ref, h_ref)
        fl_ref[...] = jnp.dot(h_ref[...], wf_ref[...], preferred_element_type=F32) + bf_ref[...]

    acc = jnp.dot(h_ref[...], w_ref[...], preferred_element_type=F32)
    acc = acc * jnp.where(j < q_tiles, q_scale, 1.0)
    qkv_ref[...] = acc.astype(BF16)


def _attn_in_proj(x2, g, sc, sh, w_qkv, w_f, b_f, rows_per_batch):
    m, d = x2.shape
    n = w_qkv.shape[1]
    tm = _pick(rows_per_batch, (1024, 512, 256, 128))
    tn = _pick(d, (1024, 512, 256, 128))
    kern = functools.partial(_attn_in_kernel, q_tiles=d // tn, q_scale=HEAD_DIM ** -0.5 * LOG2E)
    return pl.pallas_call(
        kern,
        grid=(m // tm, n // tn),
        in_specs=_row_specs(tm, d, rows_per_batch) + [
            pl.BlockSpec((d, tn), lambda i, j: (0, j)),
            pl.BlockSpec((d, LANES), lambda i, j: (0, 0)),
            pl.BlockSpec((1, LANES), lambda i, j: (0, 0)),
        ],
        out_specs=[
            pl.BlockSpec((tm, tn), lambda i, j: (i, j)),
            pl.BlockSpec((tm, LANES), lambda i, j: (i, 0)),
        ],
        out_shape=[
            jax.ShapeDtypeStruct((m, n), BF16),
            jax.ShapeDtypeStruct((m, LANES), F32),
        ],
        scratch_shapes=[pltpu.VMEM((tm, d), BF16)],
        compiler_params=_params("arbitrary", "arbitrary"),
        name="attn_in_proj",
    )(x2, g, sc, sh, w_qkv, w_f, b_f)


def _split3(x):
    hi = x.astype(BF16)
    r1 = x - hi.astype(F32)
    mid = r1.astype(BF16)
    lo = (r1 - mid.astype(F32)).astype(BF16)
    return hi, mid, lo


def _forget_cumsum_kernel(fl_ref, f_ref, ft_ref, carry_ref, *, sub):
    @pl.when(pl.program_id(1) == 0)
    def _():
        carry_ref[...] = jnp.zeros_like(carry_ref)

    tc = fl_ref.shape[0]
    row = lax.broadcasted_iota(jnp.int32, (sub, sub), 0)
    col = lax.broadcasted_iota(jnp.int32, (sub, sub), 1)
    tril = jnp.where(row >= col, 1.0, 0.0).astype(BF16)
    carry = carry_ref[0:1, :]
    for r in range(tc // sub):
        z = fl_ref[r * sub:(r + 1) * sub, :]
        lf = jnp.minimum(z, 0.0) - jnp.log1p(jnp.exp(-jnp.abs(z)))
        hi, mid, lo = _split3(lf)
        c = (jnp.dot(tril, lo, preferred_element_type=F32)
             + jnp.dot(tril, mid, preferred_element_type=F32)
             + jnp.dot(tril, hi, preferred_element_type=F32)) + carry
        c2 = c * LOG2E
        f_ref[r * sub:(r + 1) * sub, :] = c2
        ft_ref[0, :, r * sub:(r + 1) * sub] = c2.T
        carry = c[sub - 1:sub, :]
    carry_ref[0:1, :] = carry


def _forget_cumsum(flog, batch, seq):
    tc = _pick(seq, (1024, 512, 256, 128))
    sub = _pick(tc, (256, 128))
    nt = seq // tc
    return pl.pallas_call(
        functools.partial(_forget_cumsum_kernel, sub=sub),
        grid=(batch, nt),
        in_specs=[pl.BlockSpec((tc, LANES), lambda b, t: (b * nt + t, 0))],
        out_specs=[
            pl.BlockSpec((tc, LANES), lambda b, t: (b * nt + t, 0)),
            pl.BlockSpec((1, LANES, tc), lambda b, t: (b, 0, t)),
        ],
        out_shape=[
            jax.ShapeDtypeStruct((batch * seq, LANES), F32),
            jax.ShapeDtypeStruct((batch, LANES, seq), F32),
        ],
        scratch_shapes=[pltpu.VMEM((SUBLANES, LANES), F32)],
        compiler_params=_params("arbitrary", "arbitrary"),
        name="forget_cumsum",
    )(flog)


def _fox_attn_kernel(q_ref, k_ref, v_ref, f_ref, ft_ref, o_ref, *, t, hg):
    qi = pl.program_id(2)
    lane = lax.broadcasted_iota(jnp.int32, (t, LANES), 1)
    f_blk = f_ref[...]
    heads = [pl.program_id(1) * hg + g for g in range(hg)]
    qs = [q_ref[:, g * HEAD_DIM:(g + 1) * HEAD_DIM] for g in range(hg)]
    fqs = [jnp.sum(jnp.where(lane == h, f_blk, 0.0), axis=-1, keepdims=True) for h in heads]

    def scores(g, j):
        kj = k_ref[pl.ds(pl.multiple_of(j * t, t), t), g * HEAD_DIM:(g + 1) * HEAD_DIM]
        fk = ft_ref[0, heads[g], pl.ds(j, 1), :]
        return lax.dot_general(qs[g], kj, (((1,), (1,)), ((), ())), preferred_element_type=F32) - fk

    def update(g, j, s, m_prev, l_prev, acc_prev):
        vj = v_ref[pl.ds(pl.multiple_of(j * t, t), t), g * HEAD_DIM:(g + 1) * HEAD_DIM]
        m_new = jnp.maximum(m_prev, jnp.max(s, axis=-1, keepdims=True) + fqs[g])
        alpha = jnp.exp2(m_prev - m_new)
        p = jnp.exp2(s - (m_new - fqs[g]))
        l_new = alpha * l_prev + jnp.sum(p, axis=-1, keepdims=True)
        acc_new = alpha * acc_prev + jnp.dot(p.astype(BF16), vj, preferred_element_type=F32)
        return m_new, l_new, acc_new

    def body(j, carry):
        return tuple(update(g, j, scores(g, j), *carry[g]) for g in range(hg))

    init = tuple((jnp.full((t, 1), -1e30, F32), jnp.zeros((t, 1), F32), jnp.zeros((t, HEAD_DIM), F32))
                 for _ in range(hg))
    carry = lax.fori_loop(0, qi, body, init)
    row = lax.broadcasted_iota(jnp.int32, (t, t), 0)
    col = lax.broadcasted_iota(jnp.int32, (t, t), 1)
    for g in range(hg):
        s_diag = jnp.where(col <= row, scores(g, qi), -jnp.inf)
        _, l_fin, acc = update(g, qi, s_diag, *carry[g])
        o_ref[:, g * HEAD_DIM:(g + 1) * HEAD_DIM] = (acc / l_fin).astype(BF16)


def _fox_attention(qkv, f, ft4, batch, seq, d):
    heads = d // HEAD_DIM
    hg = _pick(heads, (2, 1))
    wg = hg * HEAD_DIM
    ng = heads // hg
    t = ft4.shape[-1]
    nq = seq // t
    return pl.pallas_call(
        functools.partial(_fox_attn_kernel, t=t, hg=hg),
        grid=(batch, ng, nq),
        in_specs=[
            pl.BlockSpec((t, wg), lambda b, h, i: (b * nq + i, h)),
            pl.BlockSpec((seq, wg), lambda b, h, i: (b, ng + h)),
            pl.BlockSpec((seq, wg), lambda b, h, i: (b, 2 * ng + h)),
            pl.BlockSpec((t, LANES), lambda b, h, i: (b * nq + i, 0)),
            pl.BlockSpec((1, heads, nq, t), lambda b, h, i: (b, 0, 0, 0)),
        ],
        out_specs=pl.BlockSpec((t, wg), lambda b, h, i: (b * nq + i, h)),
        out_shape=jax.ShapeDtypeStruct((batch * seq, d), BF16),
        compiler_params=_params("arbitrary", "arbitrary", "arbitrary"),
        name="fox_attention",
    )(qkv, qkv, qkv, f, ft4)


def _gelu_tanh(x):
    c = 0.7978845608028654
    return 0.5 * x * (1.0 + jnp.tanh(c * (x + 0.044715 * (x * x * x))))


def _gm_in_kernel(x_ref, g_ref, sc_ref, sh_ref, w_ref, z_ref, h_ref):
    @pl.when(pl.program_id(1) == 0)
    def _():
        _norm_modulate(x_ref, g_ref, sc_ref, sh_ref, h_ref)

    acc = jnp.dot(h_ref[...], w_ref[...], preferred_element_type=F32)
    z_ref[...] = _gelu_tanh(acc).astype(BF16)


def _gm_in_proj(x2, g, sc, sh, w, rows_per_batch):
    m, d = x2.shape
    n = w.shape[1]
    tm = _pick(rows_per_batch, (1024, 512, 256, 128))
    tn = _pick(n, (1024, 512, 256, 128))
    return pl.pallas_call(
        _gm_in_kernel,
        grid=(m // tm, n // tn),
        in_specs=_row_specs(tm, d, rows_per_batch) + [pl.BlockSpec((d, tn), lambda i, j: (0, j))],
        out_specs=pl.BlockSpec((tm, tn), lambda i, j: (i, j)),
        out_shape=jax.ShapeDtypeStruct((m, n), BF16),
        scratch_shapes=[pltpu.VMEM((tm, d), BF16)],
        compiler_params=_params("arbitrary", "arbitrary"),
        name="gmlp_in_proj",
    )(x2, g, sc, sh, w)


def _gm_gate_kernel(u_ref, v_ref, vg_ref, ws_ref, bst_ref, o_ref):
    rows, dg = v_ref.shape
    groups = dg // GM_GROUP
    v = v_ref[...].astype(F32)
    r = lax.rsqrt(jnp.mean(v * v, axis=-1, keepdims=True) + EPS)
    vn = ((v * r) * vg_ref[...]).astype(BF16)
    row = lax.broadcasted_iota(jnp.int32, (CHUNK, CHUNK), 0)
    col = lax.broadcasted_iota(jnp.int32, (CHUNK, CHUNK), 1)
    causal = col <= row
    for g in range(groups):
        w = jnp.where(causal, ws_ref[g], 0.0).astype(BF16)
        bias = bst_ref[:, g:g + 1]
        lo = g * GM_GROUP
        for c in range(rows // CHUNK):
            r0 = c * CHUNK
            sv = jnp.dot(w, vn[r0:r0 + CHUNK, lo:lo + GM_GROUP], preferred_element_type=F32) + bias
            u = u_ref[r0:r0 + CHUNK, lo:lo + GM_GROUP].astype(F32)
            o_ref[r0:r0 + CHUNK, lo:lo + GM_GROUP] = (u * sv).astype(BF16)


def _gm_gate(z, v_g, w_s, b_s_t, rows_per_batch):
    m, n2 = z.shape
    dg = n2 // 2
    groups = dg // GM_GROUP
    tr = _pick(rows_per_batch, (512, 256, 128))
    return pl.pallas_call(
        _gm_gate_kernel,
        grid=(m // tr,),
        in_specs=[
            pl.BlockSpec((tr, dg), lambda i: (i, 0)),
            pl.BlockSpec((tr, dg), lambda i: (i, 1)),
            pl.BlockSpec((1, dg), lambda i: (0, 0)),
            pl.BlockSpec((groups, CHUNK, CHUNK), lambda i: (0, 0, 0)),
            pl.BlockSpec((CHUNK, groups), lambda i: (0, 0)),
        ],
        out_specs=pl.BlockSpec((tr, dg), lambda i: (i, 0)),
        out_shape=jax.ShapeDtypeStruct((m, dg), BF16),
        compiler_params=_params("arbitrary"),
        name="gmlp_spatial_gate",
    )(z, z, v_g, w_s, b_s_t)


def _ffn_in_kernel(x_ref, g_ref, sc_ref, sh_ref, wg_ref, wu_ref, cwg_ref, cwu_ref, cbg_ref, cbu_ref,
                   o_ref, h_ref, sg_ref, su_ref, carry_g_ref, carry_u_ref, *, tiles_per_batch):
    i = pl.program_id(0)
    j = pl.program_id(1)
    tm = x_ref.shape[0]
    halo = SUBLANES
    rc = min(tm, FFN_ROW_CHUNK)

    @pl.when(j == 0)
    def _():
        _norm_modulate(x_ref, g_ref, sc_ref, sh_ref, h_ref)

    first_in_batch = (i % tiles_per_batch) == 0

    for s_ref, carry_ref in ((sg_ref, carry_g_ref), (su_ref, carry_u_ref)):
        @pl.when(first_in_batch)
        def _():
            s_ref[0:halo, :] = jnp.zeros((halo, s_ref.shape[1]), F32)

        @pl.when(jnp.logical_not(first_in_batch))
        def _():
            s_ref[0:halo, :] = carry_ref[j]

    def conv(acc, s_ref, r0, cw_ref, cb_ref):
        s_ref[halo + r0:halo + r0 + rc, :] = acc
        a1 = s_ref[halo - 1 + r0:halo - 1 + r0 + rc, :]
        a2 = s_ref[halo - 2 + r0:halo - 2 + r0 + rc, :]
        return cw_ref[0:1, :] * a2 + cw_ref[1:2, :] * a1 + cw_ref[2:3, :] * acc + cb_ref[...]

    for c in range(tm // rc):
        r0 = c * rc
        h = h_ref[r0:r0 + rc, :]
        gate = conv(jnp.dot(h, wg_ref[...], preferred_element_type=F32), sg_ref, r0, cwg_ref, cbg_ref)
        up = conv(jnp.dot(h, wu_ref[...], preferred_element_type=F32), su_ref, r0, cwu_ref, cbu_ref)
        o_ref[r0:r0 + rc, :] = (_silu(gate) * up).astype(BF16)

    carry_g_ref[j] = sg_ref[tm:tm + halo, :]
    carry_u_ref[j] = su_ref[tm:tm + halo, :]


def _ffn_in(x2, g, sc, sh, wg, wu, cwg, cwu, cbg, cbu, rows_per_batch):
    m, d = x2.shape
    fp = wg.shape[1]
    tm = _pick(rows_per_batch, (1024, 512, 256, 128))
    tn = _pick(fp, (512, 256, 128))
    nj = fp // tn
    kern = functools.partial(_ffn_in_kernel, tiles_per_batch=rows_per_batch // tm)
    col = lambda i, j: (0, j)
    return pl.pallas_call(
        kern,
        grid=(m // tm, nj),
        in_specs=_row_specs(tm, d, rows_per_batch) + [
            pl.BlockSpec((d, tn), col),
            pl.BlockSpec((d, tn), col),
            pl.BlockSpec((CONV_W, tn), col),
            pl.BlockSpec((CONV_W, tn), col),
            pl.BlockSpec((1, tn), col),
            pl.BlockSpec((1, tn), col),
        ],
        out_specs=pl.BlockSpec((tm, tn), lambda i, j: (i, j)),
        out_shape=jax.ShapeDtypeStruct((m, fp), BF16),
        scratch_shapes=[
            pltpu.VMEM((tm, d), BF16),
            pltpu.VMEM((tm + SUBLANES, tn), F32),
            pltpu.VMEM((tm + SUBLANES, tn), F32),
            pltpu.VMEM((nj, SUBLANES, tn), F32),
            pltpu.VMEM((nj, SUBLANES, tn), F32),
        ],
        compiler_params=_params("arbitrary", "arbitrary"),
        name="ffn_in_conv_gate",
    )(x2, g, sc, sh, wg, wu, cwg, cwu, cbg, cbu)


def _out_proj_kernel(a_ref, w_ref, x_ref, gate_ref, o_ref):
    y = jnp.dot(a_ref[...], w_ref[...], preferred_element_type=F32)
    o_ref[...] = x_ref[...] + gate_ref[0] * y


def _out_proj_residual(a, w, x2, gate, rows_per_batch):
    m, k = a.shape
    n = w.shape[1]
    tm = _pick(rows_per_batch, (1024, 512, 256, 128))
    tn = _pick(n, (1024, 512, 256, 128))
    if k > 2 * n:
        tn = _pick(n, (512, 256, 128))
    tiles_per_batch = rows_per_batch // tm
    return pl.pallas_call(
        _out_proj_kernel,
        grid=(m // tm, n // tn),
        in_specs=[
            pl.BlockSpec((tm, k), lambda i, j: (i, 0)),
            pl.BlockSpec((k, tn), lambda i, j: (0, j)),
            pl.BlockSpec((tm, tn), lambda i, j: (i, j)),
            pl.BlockSpec((1, 1, tn), lambda i, j: (i // tiles_per_batch, 0, j)),
        ],
        out_specs=pl.BlockSpec((tm, tn), lambda i, j: (i, j)),
        out_shape=jax.ShapeDtypeStruct((m, n), F32),
        compiler_params=_params("arbitrary", "arbitrary"),
        name="out_proj_residual",
    )(a, w, x2, gate)


def _final_norm_kernel(x_ref, g_ref, o_ref):
    x = x_ref[...]
    r = lax.rsqrt(jnp.mean(x * x, axis=-1, keepdims=True) + EPS)
    o_ref[...] = (x * r) * g_ref[...]


def _final_norm(x2, g):
    m, d = x2.shape
    tm = _pick(m, (1024, 512, 256, 128))
    return pl.pallas_call(
        _final_norm_kernel,
        grid=(m // tm,),
        in_specs=[pl.BlockSpec((tm, d), lambda i: (i, 0)), pl.BlockSpec((1, d), lambda i: (0, 0))],
        out_specs=pl.BlockSpec((tm, d), lambda i: (i, 0)),
        out_shape=jax.ShapeDtypeStruct((m, d), F32),
        compiler_params=_params("arbitrary"),
        name="final_rms_norm",
    )(x2, g)


def _pad_cols(a, n):
    return jnp.pad(a, ((0, 0), (0, n - a.shape[1])))


def kernel(x, c, mod_w, mod_b, mix_norm_g, ffn_norm_g, attn_w_in, attn_b_f, attn_w_o, gm_w_in, gm_v_g, gm_w_s,
           gm_b_s, gm_w_o, ffn_w_in, ffn_conv_w, ffn_conv_b, ffn_w_out, final_g):
    batch, seq, d = x.shape
    depth = mod_w.shape[0]
    heads = d // HEAD_DIM
    d_ff = ffn_w_in.shape[-1] // 2
    fp = _round_up(d_ff, 512)
    m = batch * seq
    assert d % LANES == 0 and seq % CHUNK == 0 and heads <= LANES
    t_attn = _pick(seq, (512, 256, 128))

    x2 = x.reshape(m, d)
    mod = _modulation(c, mod_w, mod_b).reshape(depth, batch, 6, 1, d)

    for i in range(depth):
        sh1, sc1, g1, sh2, sc2, g2 = (mod[i, :, k] for k in range(6))
        j = i // 2
        gm = mix_norm_g[i].reshape(1, d)
        if i % 2 == 0:
            w_in = attn_w_in[j]
            w_qkv = w_in[:, :3 * d].astype(BF16)
            w_f = _pad_cols(w_in[:, 3 * d:], LANES).astype(BF16)
            b_f = _pad_cols(attn_b_f[j].reshape(1, heads), LANES)
            qkv, flog = _attn_in_proj(x2, gm, sc1, sh1, w_qkv, w_f, b_f, seq)
            f, ft = _forget_cumsum(flog, batch, seq)
            ft4 = ft[:, :heads, :].reshape(batch, heads, seq // t_attn, t_attn)
            y = _fox_attention(qkv, f, ft4, batch, seq, d)
            w_o = attn_w_o[j].astype(BF16)
        else:
            z = _gm_in_proj(x2, gm, sc1, sh1, gm_w_in[j].astype(BF16), seq)
            y = _gm_gate(z, gm_v_g[j].reshape(1, -1), gm_w_s[j], gm_b_s[j].T, seq)
            w_o = gm_w_o[j].astype(BF16)
        x2 = _out_proj_residual(y, w_o, x2, g1, seq)

        w_in = ffn_w_in[i]
        wg = _pad_cols(w_in[:, :d_ff], fp).astype(BF16)
        wu = _pad_cols(w_in[:, d_ff:], fp).astype(BF16)
        cw = ffn_conv_w[i]
        cb = ffn_conv_b[i].reshape(1, 2 * d_ff)
        a = _ffn_in(x2, ffn_norm_g[i].reshape(1, d), sc2, sh2, wg, wu,
                    _pad_cols(cw[:, :d_ff], fp), _pad_cols(cw[:, d_ff:], fp),
                    _pad_cols(cb[:, :d_ff], fp), _pad_cols(cb[:, d_ff:], fp), seq)
        w_out = jnp.pad(ffn_w_out[i], ((0, fp - d_ff), (0, 0))).astype(BF16)
        x2 = _out_proj_residual(a, w_out, x2, g2, seq)

    return _final_norm(x2, final_g.reshape(1, d)).reshape(batch, seq, d)
```

```python
import functools

import jax
import jax.numpy as jnp
from jax import lax
from jax.experimental import pallas as pl
from jax.experimental.pallas import tpu as pltpu

HEAD_DIM = 128
CHUNK = 128
GM_GROUP = 128
CONV_W = 3
EPS = 1e-6
LOG2E = 1.4426950408889634
LANES = 128
SUBLANES = 8
VMEM_LIMIT_BYTES = 56 * 1024 * 1024
MAX_LOGIT_EXCESS = 60.0
FFN_ROW_CHUNK = 256

F32 = jnp.float32
BF16 = jnp.bfloat16


def _pick(n, prefs):
    for p in prefs:
        if n % p == 0:
            return p
    return n


def _round_up(n, m):
    return (n + m - 1) // m * m


def _params(*sem):
    return pltpu.CompilerParams(dimension_semantics=sem, vmem_limit_bytes=VMEM_LIMIT_BYTES)


def _silu(x):
    return x / (1.0 + jnp.exp(-x))


def _mod_kernel(c_ref, w_ref, b_ref, o_ref):
    ca = _silu(c_ref[...]).astype(BF16)
    o_ref[0] = jnp.dot(ca, w_ref[0].astype(BF16), preferred_element_type=F32) + b_ref[0]


def _modulation(c, mod_w, mod_b):
    depth, d, n = mod_w.shape
    b = c.shape[0]
    tn = _pick(n, (1024, 512, 256, 128))
    return pl.pallas_call(
        _mod_kernel,
        grid=(depth, n // tn),
        in_specs=[
            pl.BlockSpec((b, d), lambda l, j: (0, 0)),
            pl.BlockSpec((1, d, tn), lambda l, j: (l, 0, j)),
            pl.BlockSpec((1, 1, tn), lambda l, j: (l, 0, j)),
        ],
        out_specs=pl.BlockSpec((1, b, tn), lambda l, j: (l, 0, j)),
        out_shape=jax.ShapeDtypeStruct((depth, b, n), F32),
        compiler_params=_params("arbitrary", "arbitrary"),
        name="adaln_modulation",
    )(c, mod_w, mod_b.reshape(depth, 1, n))


def _norm_modulate(x_ref, g_ref, sc_ref, sh_ref, h_ref):
    x = x_ref[...]
    r = lax.rsqrt(jnp.mean(x * x, axis=-1, keepdims=True) + EPS)
    a = g_ref[...] * (1.0 + sc_ref[0])
    h_ref[...] = ((x * r) * a + sh_ref[0]).astype(BF16)


def _row_specs(tm, d, rows_per_batch):
    tiles_per_batch = rows_per_batch // tm
    return [
        pl.BlockSpec((tm, d), lambda i, j: (i, 0)),
        pl.BlockSpec((1, d), lambda i, j: (0, 0)),
        pl.BlockSpec((1, 1, d), lambda i, j: (i // tiles_per_batch, 0, 0)),
        pl.BlockSpec((1, 1, d), lambda i, j: (i // tiles_per_batch, 0, 0)),
    ]


def _attn_in_kernel(x_ref, g_ref, sc_ref, sh_ref, w_ref, wf_ref, bf_ref, qkv_ref, fl_ref, h_ref,
                    *, q_tiles, q_scale):
    j = pl.program_id(1)

    @pl.when(j == 0)
    def _():
        _norm_modulate(x_ref, g_ref, sc_ref, sh_ref, h_ref)
        fl_ref[...] = jnp.dot(h_ref[...], wf_ref[...], preferred_element_type=F32) + bf_ref[...]

    acc = jnp.dot(h_ref[...], w_ref[...], preferred_element_type=F32)
    acc = acc * jnp.where(j < q_tiles, q_scale, 1.0)
    qkv_ref[...] = acc.astype(BF16)


def _attn_in_proj(x2, g, sc, sh, w_qkv, w_f, b_f, rows_per_batch):
    m, d = x2.shape
    n = w_qkv.shape[1]
    tm = _pick(rows_per_batch, (1024, 512, 256, 128))
    tn = _pick(d, (1024, 512, 256, 128))
    kern = functools.partial(_attn_in_kernel, q_tiles=d // tn, q_scale=HEAD_DIM ** -0.5 * LOG2E)
    return pl.pallas_call(
        kern,
        grid=(m // tm, n // tn),
        in_specs=_row_specs(tm, d, rows_per_batch) + [
            pl.BlockSpec((d, tn), lambda i, j: (0, j)),
            pl.BlockSpec((d, LANES), lambda i, j: (0, 0)),
            pl.BlockSpec((1, LANES), lambda i, j: (0, 0)),
        ],
        out_specs=[
            pl.BlockSpec((tm, tn), lambda i, j: (i, j)),
            pl.BlockSpec((tm, LANES), lambda i, j: (i, 0)),
        ],
        out_shape=[
            jax.ShapeDtypeStruct((m, n), BF16),
            jax.ShapeDtypeStruct((m, LANES), F32),
        ],
        scratch_shapes=[pltpu.VMEM((tm, d), BF16)],
        compiler_params=_params("arbitrary", "arbitrary"),
        name="attn_in_proj",
    )(x2, g, sc, sh, w_qkv, w_f, b_f)


def _split3(x):
    hi = x.astype(BF16)
    r1 = x - hi.astype(F32)
    mid = r1.astype(BF16)
    lo = (r1 - mid.astype(F32)).astype(BF16)
    return hi, mid, lo


def _forget_cumsum_kernel(fl_ref, f_ref, ft_ref, carry_ref, *, sub):
    @pl.when(pl.program_id(1) == 0)
    def _():
        carry_ref[...] = jnp.zeros_like(carry_ref)

    tc = fl_ref.shape[0]
    row = lax.broadcasted_iota(jnp.int32, (sub, sub), 0)
    col = lax.broadcasted_iota(jnp.int32, (sub, sub), 1)
    tril = jnp.where(row >= col, 1.0, 0.0).astype(BF16)
    carry = carry_ref[0:1, :]
    for r in range(tc // sub):
        z = fl_ref[r * sub:(r + 1) * sub, :]
        lf = jnp.minimum(z, 0.0) - jnp.log1p(jnp.exp(-jnp.abs(z)))
        hi, mid, lo = _split3(lf)
        c = (jnp.dot(tril, lo, preferred_element_type=F32)
             + jnp.dot(tril, mid, preferred_element_type=F32)
             + jnp.dot(tril, hi, preferred_element_type=F32)) + carry
        c2 = c * LOG2E
        f_ref[r * sub:(r + 1) * sub, :] = c2
        ft_ref[0, :, r * sub:(r + 1) * sub] = c2.T
        carry = c[sub - 1:sub, :]
    carry_ref[0:1, :] = carry


def _forget_cumsum(flog, batch, seq):
    tc = _pick(seq, (1024, 512, 256, 128))
    sub = _pick(tc, (256, 128))
    nt = seq // tc
    return pl.pallas_call(
        functools.partial(_forget_cumsum_kernel, sub=sub),
        grid=(batch, nt),
        in_specs=[pl.BlockSpec((tc, LANES), lambda b, t: (b * nt + t, 0))],
        out_specs=[
            pl.BlockSpec((tc, LANES), lambda b, t: (b * nt + t, 0)),
            pl.BlockSpec((1, LANES, tc), lambda b, t: (b, 0, t)),
        ],
        out_shape=[
            jax.ShapeDtypeStruct((batch * seq, LANES), F32),
            jax.ShapeDtypeStruct((batch, LANES, seq), F32),
        ],
        scratch_shapes=[pltpu.VMEM((SUBLANES, LANES), F32)],
        compiler_params=_params("arbitrary", "arbitrary"),
        name="forget_cumsum",
    )(flog)


def _fox_attn_kernel(q_ref, k_ref, v_ref, f_ref, ft_ref, o_ref, vt_ref, fkb_ref, *, t, hg):
    qi = pl.program_id(2)
    nq = vt_ref.shape[1]
    heads = [pl.program_id(1) * hg + g for g in range(hg)]

    @pl.when(qi == 0)
    def _():
        lane = lax.broadcasted_iota(jnp.int32, (t, LANES), 1)
        for g in range(hg):
            for c in range(nq):
                rows = slice(c * t, (c + 1) * t)
                vt_ref[g, c] = v_ref[rows, g * HEAD_DIM:(g + 1) * HEAD_DIM].T
                fk = jnp.sum(jnp.where(lane == heads[g], f_ref[rows, :], 0.0), axis=-1, keepdims=True)
                fkb_ref[g, rows, :] = jnp.broadcast_to(fk, (t, LANES))

    qs = [q_ref[:, g * HEAD_DIM:(g + 1) * HEAD_DIM] for g in range(hg)]
    fqs = [ft_ref[0, heads[g], pl.ds(qi, 1), :] for g in range(hg)]

    def scores(g, j):
        start = pl.multiple_of(j * t, t)
        kj = k_ref[pl.ds(start, t), g * HEAD_DIM:(g + 1) * HEAD_DIM]
        s = lax.dot_general(kj, qs[g], (((1,), (1,)), ((), ())), preferred_element_type=F32)
        return s - jnp.tile(fkb_ref[g, pl.ds(start, t), :], (1, t // LANES))

    def update(g, j, s, m_prev, l_prev, acc_prev):
        m_new = jnp.maximum(m_prev, jnp.max(s, axis=0, keepdims=True) + fqs[g])
        alpha = jnp.exp2(m_prev - m_new)
        p = jnp.exp2(s - (m_new - fqs[g]))
        l_new = alpha * l_prev + jnp.sum(p, axis=0, keepdims=True)
        pv = jnp.dot(vt_ref[g, j], p.astype(BF16), preferred_element_type=F32)
        return m_new, l_new, alpha * acc_prev + pv

    def update_from_running_max(g, j, s, m_prev, l_prev, acc_prev):
        p = jnp.exp2(s - (m_prev - fqs[g]))
        bmax = jnp.max(s, axis=0, keepdims=True) + fqs[g]
        l_mid = l_prev + jnp.sum(p, axis=0, keepdims=True)
        acc_mid = acc_prev + jnp.dot(vt_ref[g, j], p.astype(BF16), preferred_element_type=F32)
        m_new = jnp.maximum(m_prev, bmax)
        alpha = jnp.exp2(m_prev - m_new)
        return (m_new, alpha * l_mid, alpha * acc_mid), bmax - m_prev

    def exact_body(it, carry):
        j = qi - 1 - it
        return tuple(update(g, j, scores(g, j), *carry[g]) for g in range(hg))

    def fast_body(it, carry_and_excess):
        carry, excess = carry_and_excess
        j = qi - 1 - it
        out = []
        s_next = scores(0, j)
        for g in range(hg):
            s_cur = s_next
            if g + 1 < hg:
                s_next = scores(g + 1, j)
            new_g, exc_g = update_from_running_max(g, j, s_cur, *carry[g])
            out.append(new_g)
            excess = jnp.maximum(excess, exc_g)
        return tuple(out), excess

    key = lax.broadcasted_iota(jnp.int32, (t, t), 0)
    qry = lax.broadcasted_iota(jnp.int32, (t, t), 1)
    init = (jnp.full((1, t), -1e30, F32), jnp.zeros((1, t), F32), jnp.zeros((HEAD_DIM, t), F32))
    diag = tuple(update(g, qi, jnp.where(key <= qry, scores(g, qi), -jnp.inf), *init) for g in range(hg))
    fast, excess = lax.fori_loop(0, qi, fast_body, (diag, jnp.zeros((1, t), F32)))
    final = lax.cond(jnp.max(excess) > MAX_LOGIT_EXCESS,
                     lambda: lax.fori_loop(0, qi, exact_body, diag), lambda: fast)
    for g in range(hg):
        _, l_fin, acc = final[g]
        o_ref[:, g * HEAD_DIM:(g + 1) * HEAD_DIM] = (acc / l_fin).T.astype(BF16)


def _fox_attention(qkv, f, ft4, batch, seq, d):
    heads = d // HEAD_DIM
    hg = _pick(heads, (4, 2, 1))
    wg = hg * HEAD_DIM
    ng = heads // hg
    t = ft4.shape[-1]
    nq = seq // t
    return pl.pallas_call(
        functools.partial(_fox_attn_kernel, t=t, hg=hg),
        grid=(batch, ng, nq),
        in_specs=[
            pl.BlockSpec((t, wg), lambda b, h, i: (b * nq + i, h)),
            pl.BlockSpec((seq, wg), lambda b, h, i: (b, ng + h)),
            pl.BlockSpec((seq, wg), lambda b, h, i: (b, 2 * ng + h)),
            pl.BlockSpec((seq, LANES), lambda b, h, i: (b, 0)),
            pl.BlockSpec((1, heads, nq, t), lambda b, h, i: (b, 0, 0, 0)),
        ],
        out_specs=pl.BlockSpec((t, wg), lambda b, h, i: (b * nq + i, h)),
        out_shape=jax.ShapeDtypeStruct((batch * seq, d), BF16),
        scratch_shapes=[
            pltpu.VMEM((hg, nq, HEAD_DIM, t), BF16),
            pltpu.VMEM((hg, seq, LANES), F32),
        ],
        compiler_params=_params("arbitrary", "arbitrary", "arbitrary"),
        name="fox_attention",
    )(qkv, qkv, qkv, f, ft4)


def _gelu_tanh(x):
    c = 0.7978845608028654
    return 0.5 * x * (1.0 + jnp.tanh(c * (x + 0.044715 * (x * x * x))))


def _gm_in_kernel(x_ref, g_ref, sc_ref, sh_ref, w_ref, z_ref, h_ref):
    @pl.when(pl.program_id(1) == 0)
    def _():
        _norm_modulate(x_ref, g_ref, sc_ref, sh_ref, h_ref)

    acc = jnp.dot(h_ref[...], w_ref[...], preferred_element_type=F32)
    z_ref[...] = _gelu_tanh(acc).astype(BF16)


def _gm_in_proj(x2, g, sc, sh, w, rows_per_batch):
    m, d = x2.shape
    n = w.shape[1]
    tm = _pick(rows_per_batch, (1024, 512, 256, 128))
    tn = _pick(n, (1024, 512, 256, 128))
    return pl.pallas_call(
        _gm_in_kernel,
        grid=(m // tm, n // tn),
        in_specs=_row_specs(tm, d, rows_per_batch) + [pl.BlockSpec((d, tn), lambda i, j: (0, j))],
        out_specs=pl.BlockSpec((tm, tn), lambda i, j: (i, j)),
        out_shape=jax.ShapeDtypeStruct((m, n), BF16),
        scratch_shapes=[pltpu.VMEM((tm, d), BF16)],
        compiler_params=_params("arbitrary", "arbitrary"),
        name="gmlp_in_proj",
    )(x2, g, sc, sh, w)


def _gm_gate_kernel(u_ref, v_ref, vg_ref, ws_ref, bst_ref, o_ref):
    rows, dg = v_ref.shape
    groups = dg // GM_GROUP
    v = v_ref[...].astype(F32)
    r = lax.rsqrt(jnp.mean(v * v, axis=-1, keepdims=True) + EPS)
    vn = ((v * r) * vg_ref[...]).astype(BF16)
    row = lax.broadcasted_iota(jnp.int32, (CHUNK, CHUNK), 0)
    col = lax.broadcasted_iota(jnp.int32, (CHUNK, CHUNK), 1)
    causal = col <= row
    for g in range(groups):
        w = jnp.where(causal, ws_ref[g], 0.0).astype(BF16)
        bias = bst_ref[:, g:g + 1]
        lo = g * GM_GROUP
        for c in range(rows // CHUNK):
            r0 = c * CHUNK
            sv = jnp.dot(w, vn[r0:r0 + CHUNK, lo:lo + GM_GROUP], preferred_element_type=F32) + bias
            u = u_ref[r0:r0 + CHUNK, lo:lo + GM_GROUP].astype(F32)
            o_ref[r0:r0 + CHUNK, lo:lo + GM_GROUP] = (u * sv).astype(BF16)


def _gm_gate(z, v_g, w_s, b_s_t, rows_per_batch):
    m, n2 = z.shape
    dg = n2 // 2
    groups = dg // GM_GROUP
    tr = _pick(rows_per_batch, (512, 256, 128))
    return pl.pallas_call(
        _gm_gate_kernel,
        grid=(m // tr,),
        in_specs=[
            pl.BlockSpec((tr, dg), lambda i: (i, 0)),
            pl.BlockSpec((tr, dg), lambda i: (i, 1)),
            pl.BlockSpec((1, dg), lambda i: (0, 0)),
            pl.BlockSpec((groups, CHUNK, CHUNK), lambda i: (0, 0, 0)),
            pl.BlockSpec((CHUNK, groups), lambda i: (0, 0)),
        ],
        out_specs=pl.BlockSpec((tr, dg), lambda i: (i, 0)),
        out_shape=jax.ShapeDtypeStruct((m, dg), BF16),
        compiler_params=_params("arbitrary"),
        name="gmlp_spatial_gate",
    )(z, z, v_g, w_s, b_s_t)


def _ffn_in_kernel(x_ref, g_ref, sc_ref, sh_ref, wg_ref, wu_ref, cwg_ref, cwu_ref, cbg_ref, cbu_ref,
                   o_ref, h_ref, sg_ref, su_ref, carry_g_ref, carry_u_ref, *, tiles_per_batch):
    i = pl.program_id(0)
    j = pl.program_id(1)
    tm = x_ref.shape[0]
    halo = SUBLANES
    rc = min(tm, FFN_ROW_CHUNK)

    @pl.when(j == 0)
    def _():
        _norm_modulate(x_ref, g_ref, sc_ref, sh_ref, h_ref)

    first_in_batch = (i % tiles_per_batch) == 0

    for s_ref, carry_ref in ((sg_ref, carry_g_ref), (su_ref, carry_u_ref)):
        @pl.when(first_in_batch)
        def _():
            s_ref[0:halo, :] = jnp.zeros((halo, s_ref.shape[1]), F32)

        @pl.when(jnp.logical_not(first_in_batch))
        def _():
            s_ref[0:halo, :] = carry_ref[j]

    def conv(acc, s_ref, r0, cw_ref, cb_ref):
        s_ref[halo + r0:halo + r0 + rc, :] = acc
        a1 = s_ref[halo - 1 + r0:halo - 1 + r0 + rc, :]
        a2 = s_ref[halo - 2 + r0:halo - 2 + r0 + rc, :]
        return cw_ref[0:1, :] * a2 + cw_ref[1:2, :] * a1 + cw_ref[2:3, :] * acc + cb_ref[...]

    for c in range(tm // rc):
        r0 = c * rc
        h = h_ref[r0:r0 + rc, :]
        gate = conv(jnp.dot(h, wg_ref[...], preferred_element_type=F32), sg_ref, r0, cwg_ref, cbg_ref)
        up = conv(jnp.dot(h, wu_ref[...], preferred_element_type=F32), su_ref, r0, cwu_ref, cbu_ref)
        o_ref[r0:r0 + rc, :] = (_silu(gate) * up).astype(BF16)

    carry_g_ref[j] = sg_ref[tm:tm + halo, :]
    carry_u_ref[j] = su_ref[tm:tm + halo, :]


def _ffn_in(x2, g, sc, sh, wg, wu, cwg, cwu, cbg, cbu, rows_per_batch):
    m, d = x2.shape
    fp = wg.shape[1]
    tm = _pick(rows_per_batch, (1024, 512, 256, 128))
    tn = _pick(fp, (512, 256, 128))
    nj = fp // tn
    kern = functools.partial(_ffn_in_kernel, tiles_per_batch=rows_per_batch // tm)
    col = lambda i, j: (0, j)
    return pl.pallas_call(
        kern,
        grid=(m // tm, nj),
        in_specs=_row_specs(tm, d, rows_per_batch) + [
            pl.BlockSpec((d, tn), col),
            pl.BlockSpec((d, tn), col),
            pl.BlockSpec((CONV_W, tn), col),
            pl.BlockSpec((CONV_W, tn), col),
            pl.BlockSpec((1, tn), col),
            pl.BlockSpec((1, tn), col),
        ],
        out_specs=pl.BlockSpec((tm, tn), lambda i, j: (i, j)),
        out_shape=jax.ShapeDtypeStruct((m, fp), BF16),
        scratch_shapes=[
            pltpu.VMEM((tm, d), BF16),
            pltpu.VMEM((tm + SUBLANES, tn), F32),
            pltpu.VMEM((tm + SUBLANES, tn), F32),
            pltpu.VMEM((nj, SUBLANES, tn), F32),
            pltpu.VMEM((nj, SUBLANES, tn), F32),
        ],
        compiler_params=_params("arbitrary", "arbitrary"),
        name="ffn_in_conv_gate",
    )(x2, g, sc, sh, wg, wu, cwg, cwu, cbg, cbu)


def _out_proj_kernel(a_ref, w_ref, x_ref, gate_ref, o_ref):
    y = jnp.dot(a_ref[...], w_ref[...], preferred_element_type=F32)
    o_ref[...] = x_ref[...] + gate_ref[0] * y


def _out_proj_residual(a, w, x2, gate, rows_per_batch):
    m, k = a.shape
    n = w.shape[1]
    tm = _pick(rows_per_batch, (1024, 512, 256, 128))
    tn = _pick(n, (1024, 512, 256, 128))
    if k > 2 * n:
        tn = _pick(n, (512, 256, 128))
    tiles_per_batch = rows_per_batch // tm
    return pl.pallas_call(
        _out_proj_kernel,
        grid=(m // tm, n // tn),
        in_specs=[
            pl.BlockSpec((tm, k), lambda i, j: (i, 0)),
            pl.BlockSpec((k, tn), lambda i, j: (0, j)),
            pl.BlockSpec((tm, tn), lambda i, j: (i, j)),
            pl.BlockSpec((1, 1, tn), lambda i, j: (i // tiles_per_batch, 0, j)),
        ],
        out_specs=pl.BlockSpec((tm, tn), lambda i, j: (i, j)),
        out_shape=jax.ShapeDtypeStruct((m, n), F32),
        compiler_params=_params("arbitrary", "arbitrary"),
        name="out_proj_residual",
    )(a, w, x2, gate)


def _final_norm_kernel(x_ref, g_ref, o_ref):
    x = x_ref[...]
    r = lax.rsqrt(jnp.mean(x * x, axis=-1, keepdims=True) + EPS)
    o_ref[...] = (x * r) * g_ref[...]


def _final_norm(x2, g):
    m, d = x2.shape
    tm = _pick(m, (1024, 512, 256, 128))
    return pl.pallas_call(
        _final_norm_kernel,
        grid=(m // tm,),
        in_specs=[pl.BlockSpec((tm, d), lambda i: (i, 0)), pl.BlockSpec((1, d), lambda i: (0, 0))],
        out_specs=pl.BlockSpec((tm, d), lambda i: (i, 0)),
        out_shape=jax.ShapeDtypeStruct((m, d), F32),
        compiler_params=_params("arbitrary"),
        name="final_rms_norm",
    )(x2, g)


def _pad_cols(a, n):
    return jnp.pad(a, ((0, 0), (0, n - a.shape[1])))


def kernel(x, c, mod_w, mod_b, mix_norm_g, ffn_norm_g, attn_w_in, attn_b_f, attn_w_o, gm_w_in, gm_v_g, gm_w_s,
           gm_b_s, gm_w_o, ffn_w_in, ffn_conv_w, ffn_conv_b, ffn_w_out, final_g):
    batch, seq, d = x.shape
    depth = mod_w.shape[0]
    heads = d // HEAD_DIM
    d_ff = ffn_w_in.shape[-1] // 2
    fp = _round_up(d_ff, 512)
    m = batch * seq
    assert d % LANES == 0 and seq % CHUNK == 0 and heads <= LANES
    t_attn = _pick(seq, (512, 256, 128))

    x2 = x.reshape(m, d)
    mod = _modulation(c, mod_w, mod_b).reshape(depth, batch, 6, 1, d)

    for i in range(depth):
        sh1, sc1, g1, sh2, sc2, g2 = (mod[i, :, k] for k in range(6))
        j = i // 2
        gm = mix_norm_g[i].reshape(1, d)
        if i % 2 == 0:
            w_in = attn_w_in[j]
            w_qkv = w_in[:, :3 * d].astype(BF16)
            w_f = _pad_cols(w_in[:, 3 * d:], LANES).astype(BF16)
            b_f = _pad_cols(attn_b_f[j].reshape(1, heads), LANES)
            qkv, flog = _attn_in_proj(x2, gm, sc1, sh1, w_qkv, w_f, b_f, seq)
            f, ft = _forget_cumsum(flog, batch, seq)
            ft4 = ft[:, :heads, :].reshape(batch, heads, seq // t_attn, t_attn)
            y = _fox_attention(qkv, f, ft4, batch, seq, d)
            w_o = attn_w_o[j].astype(BF16)
        else:
            z = _gm_in_proj(x2, gm, sc1, sh1, gm_w_in[j].astype(BF16), seq)
            y = _gm_gate(z, gm_v_g[j].reshape(1, -1), gm_w_s[j], gm_b_s[j].T, seq)
            w_o = gm_w_o[j].astype(BF16)
        x2 = _out_proj_residual(y, w_o, x2, g1, seq)

        w_in = ffn_w_in[i]
        wg = _pad_cols(w_in[:, :d_ff], fp).astype(BF16)
        wu = _pad_cols(w_in[:, d_ff:], fp).astype(BF16)
        cw = ffn_conv_w[i]
        cb = ffn_conv_b[i].reshape(1, 2 * d_ff)
        a = _ffn_in(x2, ffn_norm_g[i].reshape(1, d), sc2, sh2, wg, wu,
                    _pad_cols(cw[:, :d_ff], fp), _pad_cols(cw[:, d_ff:], fp),
                    _pad_cols(cb[:, :d_ff], fp), _pad_cols(cb[:, d_ff:], fp), seq)
        w_out = jnp.pad(ffn_w_out[i], ((0, fp - d_ff), (0, 0))).astype(BF16)
        x2 = _out_proj_residual(a, w_out, x2, g2, seq)

    return _final_norm(x2, final_g.reshape(1, d)).reshape(batch, seq, d)
```

```python
import functools

import jax
import jax.numpy as jnp
from jax import lax
from jax.experimental import pallas as pl
from jax.experimental.pallas import tpu as pltpu

HEAD_DIM = 128
CHUNK = 128
GM_GROUP = 128
CONV_W = 3
EPS = 1e-6
LOG2E = 1.4426950408889634
LANES = 128
SUBLANES = 8
VMEM_LIMIT_BYTES = 56 * 1024 * 1024
RESIDENT_WEIGHT_BYTES = 16 * 1024 * 1024
MAX_LOGIT_EXCESS = 60.0
FFN_ROW_CHUNK = 256

F32 = jnp.float32
BF16 = jnp.bfloat16


def _pick(n, prefs):
    for p in prefs:
        if n % p == 0:
            return p
    return n


def _round_up(n, m):
    return (n + m - 1) // m * m


def _params(*sem):
    return pltpu.CompilerParams(dimension_semantics=sem, vmem_limit_bytes=VMEM_LIMIT_BYTES)


def _silu(x):
    return x / (1.0 + jnp.exp(-x))


def _mod_kernel(c_ref, w_ref, b_ref, o_ref):
    ca = _silu(c_ref[...]).astype(BF16)
    o_ref[0] = jnp.dot(ca, w_ref[0].astype(BF16), preferred_element_type=F32) + b_ref[0]


def _modulation(c, mod_w, mod_b):
    depth, d, n = mod_w.shape
    b = c.shape[0]
    tn = _pick(n, (1024, 512, 256, 128))
    return pl.pallas_call(
        _mod_kernel,
        grid=(depth, n // tn),
        in_specs=[
            pl.BlockSpec((b, d), lambda l, j: (0, 0)),
            pl.BlockSpec((1, d, tn), lambda l, j: (l, 0, j)),
            pl.BlockSpec((1, 1, tn), lambda l, j: (l, 0, j)),
        ],
        out_specs=pl.BlockSpec((1, b, tn), lambda l, j: (l, 0, j)),
        out_shape=jax.ShapeDtypeStruct((depth, b, n), F32),
        compiler_params=_params("arbitrary", "arbitrary"),
        name="adaln_modulation",
    )(c, mod_w, mod_b.reshape(depth, 1, n))


def _norm_modulate(x_ref, g_ref, sc_ref, sh_ref, h_ref):
    x = x_ref[...]
    r = lax.rsqrt(jnp.mean(x * x, axis=-1, keepdims=True) + EPS)
    a = g_ref[...] * (1.0 + sc_ref[0])
    h_ref[...] = ((x * r) * a + sh_ref[0]).astype(BF16)


def _row_specs(tm, d, rows_per_batch):
    tiles_per_batch = rows_per_batch // tm
    return [
        pl.BlockSpec((tm, d), lambda i, j: (i, 0)),
        pl.BlockSpec((1, d), lambda i, j: (0, 0)),
        pl.BlockSpec((1, 1, d), lambda i, j: (i // tiles_per_batch, 0, 0)),
        pl.BlockSpec((1, 1, d), lambda i, j: (i // tiles_per_batch, 0, 0)),
    ]


def _attn_in_kernel(x_ref, g_ref, sc_ref, sh_ref, w_ref, wf_ref, bf_ref, qkv_ref, fl_ref, h_ref,
                    *, q_tiles, q_scale):
    j = pl.program_id(1)

    @pl.when(j == 0)
    def _():
        _norm_modulate(x_ref, g_ref, sc_ref, sh_ref, h_ref)
        fl_ref[...] = jnp.dot(h_ref[...], wf_ref[...], preferred_element_type=F32) + bf_ref[...]

    acc = jnp.dot(h_ref[...], w_ref[...], preferred_element_type=F32)
    acc = acc * jnp.where(j < q_tiles, q_scale, 1.0)
    qkv_ref[...] = acc.astype(BF16)


def _attn_in_proj(x2, g, sc, sh, w_qkv, w_f, b_f, rows_per_batch):
    m, d = x2.shape
    n = w_qkv.shape[1]
    tm = _pick(rows_per_batch, (1024, 512, 256, 128))
    tn = _pick(d, (1024, 512, 256, 128))
    kern = functools.partial(_attn_in_kernel, q_tiles=d // tn, q_scale=HEAD_DIM ** -0.5 * LOG2E)
    return pl.pallas_call(
        kern,
        grid=(m // tm, n // tn),
        in_specs=_row_specs(tm, d, rows_per_batch) + [
            pl.BlockSpec((d, tn), lambda i, j: (0, j)),
            pl.BlockSpec((d, LANES), lambda i, j: (0, 0)),
            pl.BlockSpec((1, LANES), lambda i, j: (0, 0)),
        ],
        out_specs=[
            pl.BlockSpec((tm, tn), lambda i, j: (i, j)),
            pl.BlockSpec((tm, LANES), lambda i, j: (i, 0)),
        ],
        out_shape=[
            jax.ShapeDtypeStruct((m, n), BF16),
            jax.ShapeDtypeStruct((m, LANES), F32),
        ],
        scratch_shapes=[pltpu.VMEM((tm, d), BF16)],
        compiler_params=_params("arbitrary", "arbitrary"),
        name="attn_in_proj",
    )(x2, g, sc, sh, w_qkv, w_f, b_f)


def _split3(x):
    hi = x.astype(BF16)
    r1 = x - hi.astype(F32)
    mid = r1.astype(BF16)
    lo = (r1 - mid.astype(F32)).astype(BF16)
    return hi, mid, lo


def _forget_cumsum_kernel(fl_ref, f_ref, ft_ref, carry_ref, *, sub):
    @pl.when(pl.program_id(1) == 0)
    def _():
        carry_ref[...] = jnp.zeros_like(carry_ref)

    tc = fl_ref.shape[0]
    row = lax.broadcasted_iota(jnp.int32, (sub, sub), 0)
    col = lax.broadcasted_iota(jnp.int32, (sub, sub), 1)
    tril = jnp.where(row >= col, 1.0, 0.0).astype(BF16)
    carry = carry_ref[0:1, :]
    for r in range(tc // sub):
        z = fl_ref[r * sub:(r + 1) * sub, :]
        lf = jnp.minimum(z, 0.0) - jnp.log1p(jnp.exp(-jnp.abs(z)))
        hi, mid, lo = _split3(lf)
        c = (jnp.dot(tril, lo, preferred_element_type=F32)
             + jnp.dot(tril, mid, preferred_element_type=F32)
             + jnp.dot(tril, hi, preferred_element_type=F32)) + carry
        c2 = c * LOG2E
        f_ref[r * sub:(r + 1) * sub, :] = c2
        ft_ref[0, :, r * sub:(r + 1) * sub] = c2.T
        carry = c[sub - 1:sub, :]
    carry_ref[0:1, :] = carry


def _forget_cumsum(flog, batch, seq):
    tc = _pick(seq, (1024, 512, 256, 128))
    sub = _pick(tc, (256, 128))
    nt = seq // tc
    return pl.pallas_call(
        functools.partial(_forget_cumsum_kernel, sub=sub),
        grid=(batch, nt),
        in_specs=[pl.BlockSpec((tc, LANES), lambda b, t: (b * nt + t, 0))],
        out_specs=[
            pl.BlockSpec((tc, LANES), lambda b, t: (b * nt + t, 0)),
            pl.BlockSpec((1, LANES, tc), lambda b, t: (b, 0, t)),
        ],
        out_shape=[
            jax.ShapeDtypeStruct((batch * seq, LANES), F32),
            jax.ShapeDtypeStruct((batch, LANES, seq), F32),
        ],
        scratch_shapes=[pltpu.VMEM((SUBLANES, LANES), F32)],
        compiler_params=_params("arbitrary", "arbitrary"),
        name="forget_cumsum",
    )(flog)


def _fox_attn_kernel(q_ref, k_ref, v_ref, f_ref, ft_ref, o_ref, vt_ref, fkb_ref, m_ref, l_ref, acc_ref, exc_ref,
                     *, t, hg):
    qi = pl.program_id(2)
    nq = vt_ref.shape[1]
    heads = [pl.program_id(1) * hg + g for g in range(hg)]

    @pl.when(qi == 0)
    def _():
        lane = lax.broadcasted_iota(jnp.int32, (t, LANES), 1)
        for g in range(hg):
            for c in range(nq):
                rows = slice(c * t, (c + 1) * t)
                vt_ref[g, c] = v_ref[rows, g * HEAD_DIM:(g + 1) * HEAD_DIM].T
                fk = jnp.sum(jnp.where(lane == heads[g], f_ref[rows, :], 0.0), axis=-1, keepdims=True)
                fkb_ref[g, rows, :] = jnp.broadcast_to(fk, (t, LANES))

    qs = [q_ref[:, g * HEAD_DIM:(g + 1) * HEAD_DIM] for g in range(hg)]
    fqs = [ft_ref[0, heads[g], pl.ds(qi, 1), :] for g in range(hg)]

    def scores(g, j):
        start = pl.multiple_of(j * t, t)
        kj = k_ref[pl.ds(start, t), g * HEAD_DIM:(g + 1) * HEAD_DIM]
        s = lax.dot_general(kj, qs[g], (((1,), (1,)), ((), ())), preferred_element_type=F32)
        return s - jnp.tile(fkb_ref[g, pl.ds(start, t), :], (1, t // LANES))

    def update(g, j, s, m_prev, l_prev, acc_prev):
        m_new = jnp.maximum(m_prev, jnp.max(s, axis=0, keepdims=True) + fqs[g])
        alpha = jnp.exp2(m_prev - m_new)
        p = jnp.exp2(s - (m_new - fqs[g]))
        l_new = alpha * l_prev + jnp.sum(p, axis=0, keepdims=True)
        pv = jnp.dot(vt_ref[g, j], p.astype(BF16), preferred_element_type=F32)
        return m_new, l_new, alpha * acc_prev + pv

    def update_from_running_max(g, j, s, m_prev, l_prev, acc_prev):
        p = jnp.exp2(s - (m_prev - fqs[g]))
        bmax = jnp.max(s, axis=0, keepdims=True) + fqs[g]
        l_mid = l_prev + jnp.sum(p, axis=0, keepdims=True)
        acc_mid = acc_prev + jnp.dot(vt_ref[g, j], p.astype(BF16), preferred_element_type=F32)
        m_new = jnp.maximum(m_prev, bmax)
        alpha = jnp.exp2(m_prev - m_new)
        return (m_new, alpha * l_mid, alpha * acc_mid), bmax - m_prev

    def load_state(g):
        return m_ref[g], l_ref[g], acc_ref[g]

    def store_state(g, state):
        m_ref[g], l_ref[g], acc_ref[g] = state

    def exact_body(it, _):
        j = qi - 1 - it
        for g in range(hg):
            store_state(g, update(g, j, scores(g, j), *load_state(g)))
        return 0

    def fast_body(it, _):
        j = qi - 1 - it
        excess = exc_ref[...]
        s_next = scores(0, j)
        for g in range(hg):
            s_cur = s_next
            if g + 1 < hg:
                s_next = scores(g + 1, j)
            new_g, exc_g = update_from_running_max(g, j, s_cur, *load_state(g))
            store_state(g, new_g)
            excess = jnp.maximum(excess, exc_g)
        exc_ref[...] = excess
        return 0

    half = t // 2
    causal = (lax.broadcasted_iota(jnp.int32, (half, half), 0) <= lax.broadcasted_iota(jnp.int32, (half, half), 1))

    def diag_scores(g):
        start = pl.multiple_of(qi * t, t)
        cols = slice(g * HEAD_DIM, (g + 1) * HEAD_DIM)
        nt = (((1,), (1,)), ((), ()))
        sa = lax.dot_general(k_ref[pl.ds(start, half), cols], qs[g], nt, preferred_element_type=F32)
        sa = sa - jnp.tile(fkb_ref[g, pl.ds(start, half), :], (1, t // LANES))
        sb = lax.dot_general(k_ref[pl.ds(start + half, half), cols], qs[g][half:, :], nt,
                             preferred_element_type=F32)
        sb = sb - jnp.tile(fkb_ref[g, pl.ds(start + half, half), :], (1, half // LANES))
        sa = jnp.concatenate([jnp.where(causal, sa[:, :half], -jnp.inf), sa[:, half:]], axis=1)
        return sa, jnp.where(causal, sb, -jnp.inf)

    def late(x_all, x_late, op):
        return jnp.concatenate([x_all[:, :half], op(x_all[:, half:], x_late)], axis=1)

    def diag_update(g, sa, sb):
        m_new = late(jnp.max(sa, axis=0, keepdims=True), jnp.max(sb, axis=0, keepdims=True), jnp.maximum) + fqs[g]
        shift = m_new - fqs[g]
        pa = jnp.exp2(sa - shift)
        pb = jnp.exp2(sb - shift[:, half:])
        l_new = late(jnp.sum(pa, axis=0, keepdims=True), jnp.sum(pb, axis=0, keepdims=True), jnp.add)
        vt = vt_ref[g, qi]
        acc_a = jnp.dot(vt[:, :half], pa.astype(BF16), preferred_element_type=F32)
        acc_b = jnp.dot(vt[:, half:], pb.astype(BF16), preferred_element_type=F32)
        return m_new, l_new, late(acc_a, acc_b, jnp.add)

    def diag_block():
        s_diag = [diag_scores(g) for g in range(hg)]
        for g in range(hg):
            store_state(g, diag_update(g, *s_diag[g]))

    diag_block()
    exc_ref[...] = jnp.zeros_like(exc_ref)
    lax.fori_loop(0, qi, fast_body, 0)

    @pl.when(jnp.max(exc_ref[...]) > MAX_LOGIT_EXCESS)
    def _():
        diag_block()
        lax.fori_loop(0, qi, exact_body, 0)

    for g in range(hg):
        o_ref[:, g * HEAD_DIM:(g + 1) * HEAD_DIM] = (acc_ref[g] / l_ref[g]).T.astype(BF16)


def _fox_attention(qkv, f, ft4, batch, seq, d):
    heads = d // HEAD_DIM
    hg = _pick(heads, (4, 2, 1))
    wg = hg * HEAD_DIM
    ng = heads // hg
    t = ft4.shape[-1]
    nq = seq // t
    return pl.pallas_call(
        functools.partial(_fox_attn_kernel, t=t, hg=hg),
        grid=(batch, ng, nq),
        in_specs=[
            pl.BlockSpec((t, wg), lambda b, h, i: (b * nq + i, h)),
            pl.BlockSpec((seq, wg), lambda b, h, i: (b, ng + h)),
            pl.BlockSpec((seq, wg), lambda b, h, i: (b, 2 * ng + h)),
            pl.BlockSpec((seq, LANES), lambda b, h, i: (b, 0)),
            pl.BlockSpec((1, heads, nq, t), lambda b, h, i: (b, 0, 0, 0)),
        ],
        out_specs=pl.BlockSpec((t, wg), lambda b, h, i: (b * nq + i, h)),
        out_shape=jax.ShapeDtypeStruct((batch * seq, d), BF16),
        scratch_shapes=[
            pltpu.VMEM((hg, nq, HEAD_DIM, t), BF16),
            pltpu.VMEM((hg, seq, LANES), F32),
            pltpu.VMEM((hg, 1, t), F32),
            pltpu.VMEM((hg, 1, t), F32),
            pltpu.VMEM((hg, HEAD_DIM, t), F32),
            pltpu.VMEM((1, t), F32),
        ],
        compiler_params=_params("arbitrary", "arbitrary", "arbitrary"),
        name="fox_attention",
    )(qkv, qkv, qkv, f, ft4)


def _gelu_tanh(x):
    c = 0.7978845608028654
    return 0.5 * x * (1.0 + jnp.tanh(c * (x + 0.044715 * (x * x * x))))


def _gm_in_kernel(x_ref, g_ref, sc_ref, sh_ref, w_ref, z_ref, h_ref):
    @pl.when(pl.program_id(1) == 0)
    def _():
        _norm_modulate(x_ref, g_ref, sc_ref, sh_ref, h_ref)

    acc = jnp.dot(h_ref[...], w_ref[...], preferred_element_type=F32)
    z_ref[...] = _gelu_tanh(acc).astype(BF16)


def _gm_in_proj(x2, g, sc, sh, w, rows_per_batch):
    m, d = x2.shape
    n = w.shape[1]
    tm = _pick(rows_per_batch, (1024, 512, 256, 128))
    tn = _pick(n, (1024, 512, 256, 128))
    return pl.pallas_call(
        _gm_in_kernel,
        grid=(m // tm, n // tn),
        in_specs=_row_specs(tm, d, rows_per_batch) + [pl.BlockSpec((d, tn), lambda i, j: (0, j))],
        out_specs=pl.BlockSpec((tm, tn), lambda i, j: (i, j)),
        out_shape=jax.ShapeDtypeStruct((m, n), BF16),
        scratch_shapes=[pltpu.VMEM((tm, d), BF16)],
        compiler_params=_params("arbitrary", "arbitrary"),
        name="gmlp_in_proj",
    )(x2, g, sc, sh, w)


def _gm_gate_kernel(u_ref, v_ref, vg_ref, ws_ref, bst_ref, o_ref):
    rows, dg = v_ref.shape
    groups = dg // GM_GROUP
    v = v_ref[...].astype(F32)
    r = lax.rsqrt(jnp.mean(v * v, axis=-1, keepdims=True) + EPS)
    vn = ((v * r) * vg_ref[...]).astype(BF16)
    row = lax.broadcasted_iota(jnp.int32, (CHUNK, CHUNK), 0)
    col = lax.broadcasted_iota(jnp.int32, (CHUNK, CHUNK), 1)
    causal = col <= row
    for g in range(groups):
        w = jnp.where(causal, ws_ref[g], 0.0).astype(BF16)
        bias = bst_ref[:, g:g + 1]
        lo = g * GM_GROUP
        for c in range(rows // CHUNK):
            r0 = c * CHUNK
            sv = jnp.dot(w, vn[r0:r0 + CHUNK, lo:lo + GM_GROUP], preferred_element_type=F32) + bias
            u = u_ref[r0:r0 + CHUNK, lo:lo + GM_GROUP].astype(F32)
            o_ref[r0:r0 + CHUNK, lo:lo + GM_GROUP] = (u * sv).astype(BF16)


def _gm_gate(z, v_g, w_s, b_s_t, rows_per_batch):
    m, n2 = z.shape
    dg = n2 // 2
    groups = dg // GM_GROUP
    tr = _pick(rows_per_batch, (512, 256, 128))
    return pl.pallas_call(
        _gm_gate_kernel,
        grid=(m // tr,),
        in_specs=[
            pl.BlockSpec((tr, dg), lambda i: (i, 0)),
            pl.BlockSpec((tr, dg), lambda i: (i, 1)),
            pl.BlockSpec((1, dg), lambda i: (0, 0)),
            pl.BlockSpec((groups, CHUNK, CHUNK), lambda i: (0, 0, 0)),
            pl.BlockSpec((CHUNK, groups), lambda i: (0, 0)),
        ],
        out_specs=pl.BlockSpec((tr, dg), lambda i: (i, 0)),
        out_shape=jax.ShapeDtypeStruct((m, dg), BF16),
        compiler_params=_params("arbitrary"),
        name="gmlp_spatial_gate",
    )(z, z, v_g, w_s, b_s_t)


def _ffn_in_kernel(x_ref, g_ref, sc_ref, sh_ref, wg_ref, wu_ref, cwg_ref, cwu_ref, cbg_ref, cbu_ref,
                   o_ref, h_ref, carry_g_ref, carry_u_ref, *, tiles_per_batch):
    i = pl.program_id(0)
    j = pl.program_id(1)
    tm = x_ref.shape[0]
    halo = SUBLANES
    rc = min(tm, FFN_ROW_CHUNK)

    @pl.when(j == 0)
    def _():
        _norm_modulate(x_ref, g_ref, sc_ref, sh_ref, h_ref)

    @pl.when((i % tiles_per_batch) == 0)
    def _():
        carry_g_ref[j] = jnp.zeros(carry_g_ref.shape[1:], F32)
        carry_u_ref[j] = jnp.zeros(carry_u_ref.shape[1:], F32)

    sub = lax.broadcasted_iota(jnp.int32, (halo, wg_ref.shape[1]), 0)

    def shifted(acc, tail, k):
        moved = pltpu.roll(acc, k, 0)
        head = jnp.where(sub < k, pltpu.roll(tail, k, 0), moved[0:halo, :])
        return jnp.concatenate([head, moved[halo:, :]], axis=0)

    def conv(acc, tail, cw_ref, cb_ref):
        return (cw_ref[0:1, :] * shifted(acc, tail, 2) + cw_ref[1:2, :] * shifted(acc, tail, 1)
                + cw_ref[2:3, :] * acc + cb_ref[...])

    tail_g = carry_g_ref[j]
    tail_u = carry_u_ref[j]
    sizes = [rc] * (tm // rc)
    if rc // 2 >= 128 and rc % 16 == 0:
        sizes = sizes[:-1] + [rc // 2, rc // 2]
    r0 = 0
    for rc in sizes:
        h = h_ref[r0:r0 + rc, :]
        acc_g = jnp.dot(h, wg_ref[...], preferred_element_type=F32)
        acc_u = jnp.dot(h, wu_ref[...], preferred_element_type=F32)
        gate = conv(acc_g, tail_g, cwg_ref, cbg_ref)
        up = conv(acc_u, tail_u, cwu_ref, cbu_ref)
        o_ref[r0:r0 + rc, :] = (_silu(gate) * up).astype(BF16)
        tail_g = acc_g[rc - halo:rc, :]
        tail_u = acc_u[rc - halo:rc, :]
        r0 += rc

    carry_g_ref[j] = tail_g
    carry_u_ref[j] = tail_u


def _ffn_in(x2, g, sc, sh, wg, wu, cwg, cwu, cbg, cbu, rows_per_batch):
    m, d = x2.shape
    fp = wg.shape[1]
    tm = _pick(rows_per_batch, (1024, 512, 256, 128))
    tn = _pick(fp, (512, 256, 128))
    nj = fp // tn
    kern = functools.partial(_ffn_in_kernel, tiles_per_batch=rows_per_batch // tm)
    col = lambda i, j: (0, j)
    return pl.pallas_call(
        kern,
        grid=(m // tm, nj),
        in_specs=_row_specs(tm, d, rows_per_batch) + [
            pl.BlockSpec((d, tn), col),
            pl.BlockSpec((d, tn), col),
            pl.BlockSpec((CONV_W, tn), col),
            pl.BlockSpec((CONV_W, tn), col),
            pl.BlockSpec((1, tn), col),
            pl.BlockSpec((1, tn), col),
        ],
        out_specs=pl.BlockSpec((tm, tn), lambda i, j: (i, j)),
        out_shape=jax.ShapeDtypeStruct((m, fp), BF16),
        scratch_shapes=[
            pltpu.VMEM((tm, d), BF16),
            pltpu.VMEM((nj, SUBLANES, tn), F32),
            pltpu.VMEM((nj, SUBLANES, tn), F32),
        ],
        compiler_params=_params("arbitrary", "arbitrary"),
        name="ffn_in_conv_gate",
    )(x2, g, sc, sh, wg, wu, cwg, cwu, cbg, cbu)


def _out_proj_kernel(a_ref, w_ref, x_ref, gate_ref, o_ref):
    y = jnp.dot(a_ref[...], w_ref[...], preferred_element_type=F32)
    o_ref[...] = x_ref[...] + gate_ref[0] * y


def _out_proj_residual(a, w, x2, gate, rows_per_batch):
    m, k = a.shape
    n = w.shape[1]
    tm = _pick(rows_per_batch, (1024, 512, 256, 128))
    tn = _pick(n, (1024, 512, 256, 128))
    if k > 2 * n:
        tn = _pick(n, (512, 256, 128))
    elif 2 * k * n * 2 <= RESIDENT_WEIGHT_BYTES:
        tn = n
        tm = _pick(rows_per_batch, (512, 256, 128))
    tiles_per_batch = rows_per_batch // tm
    return pl.pallas_call(
        _out_proj_kernel,
        grid=(m // tm, n // tn),
        in_specs=[
            pl.BlockSpec((tm, k), lambda i, j: (i, 0)),
            pl.BlockSpec((k, tn), lambda i, j: (0, j)),
            pl.BlockSpec((tm, tn), lambda i, j: (i, j)),
            pl.BlockSpec((1, 1, tn), lambda i, j: (i // tiles_per_batch, 0, j)),
        ],
        out_specs=pl.BlockSpec((tm, tn), lambda i, j: (i, j)),
        out_shape=jax.ShapeDtypeStruct((m, n), F32),
        compiler_params=_params("arbitrary", "arbitrary"),
        name="out_proj_residual",
    )(a, w, x2, gate)


def _final_norm_kernel(x_ref, g_ref, o_ref):
    x = x_ref[...]
    r = lax.rsqrt(jnp.mean(x * x, axis=-1, keepdims=True) + EPS)
    o_ref[...] = (x * r) * g_ref[...]


def _final_norm(x2, g):
    m, d = x2.shape
    tm = _pick(m, (1024, 512, 256, 128))
    return pl.pallas_call(
        _final_norm_kernel,
        grid=(m // tm,),
        in_specs=[pl.BlockSpec((tm, d), lambda i: (i, 0)), pl.BlockSpec((1, d), lambda i: (0, 0))],
        out_specs=pl.BlockSpec((tm, d), lambda i: (i, 0)),
        out_shape=jax.ShapeDtypeStruct((m, d), F32),
        compiler_params=_params("arbitrary"),
        name="final_rms_norm",
    )(x2, g)


def _pad_cols(a, n):
    return jnp.pad(a, ((0, 0), (0, n - a.shape[1])))


def kernel(x, c, mod_w, mod_b, mix_norm_g, ffn_norm_g, attn_w_in, attn_b_f, attn_w_o, gm_w_in, gm_v_g, gm_w_s,
           gm_b_s, gm_w_o, ffn_w_in, ffn_conv_w, ffn_conv_b, ffn_w_out, final_g):
    batch, seq, d = x.shape
    depth = mod_w.shape[0]
    heads = d // HEAD_DIM
    d_ff = ffn_w_in.shape[-1] // 2
    fp = _round_up(d_ff, 512)
    m = batch * seq
    assert d % LANES == 0 and seq % CHUNK == 0 and heads <= LANES
    t_attn = _pick(seq, (512, 256, 128))

    x2 = x.reshape(m, d)
    mod = _modulation(c, mod_w, mod_b).reshape(depth, batch, 6, 1, d)

    for i in range(depth):
        sh1, sc1, g1, sh2, sc2, g2 = (mod[i, :, k] for k in range(6))
        j = i // 2
        gm = mix_norm_g[i].reshape(1, d)
        if i % 2 == 0:
            w_in = attn_w_in[j]
            w_qkv = w_in[:, :3 * d].astype(BF16)
            w_f = _pad_cols(w_in[:, 3 * d:], LANES).astype(BF16)
            b_f = _pad_cols(attn_b_f[j].reshape(1, heads), LANES)
            qkv, flog = _attn_in_proj(x2, gm, sc1, sh1, w_qkv, w_f, b_f, seq)
            f, ft = _forget_cumsum(flog, batch, seq)
            ft4 = ft[:, :heads, :].reshape(batch, heads, seq // t_attn, t_attn)
            y = _fox_attention(qkv, f, ft4, batch, seq, d)
            w_o = attn_w_o[j].astype(BF16)
        else:
            z = _gm_in_proj(x2, gm, sc1, sh1, gm_w_in[j].astype(BF16), seq)
            y = _gm_gate(z, gm_v_g[j].reshape(1, -1), gm_w_s[j], gm_b_s[j].T, seq)
            w_o = gm_w_o[j].astype(BF16)
        x2 = _out_proj_residual(y, w_o, x2, g1, seq)

        w_in = ffn_w_in[i]
        wg = _pad_cols(w_in[:, :d_ff], fp).astype(BF16)
        wu = _pad_cols(w_in[:, d_ff:], fp).astype(BF16)
        cw = ffn_conv_w[i]
        cb = ffn_conv_b[i].reshape(1, 2 * d_ff)
        a = _ffn_in(x2, ffn_norm_g[i].reshape(1, d), sc2, sh2, wg, wu,
                    _pad_cols(cw[:, :d_ff], fp), _pad_cols(cw[:, d_ff:], fp),
                    _pad_cols(cb[:, :d_ff], fp), _pad_cols(cb[:, d_ff:], fp), seq)
        w_out = jnp.pad(ffn_w_out[i], ((0, fp - d_ff), (0, 0))).astype(BF16)
        x2 = _out_proj_residual(a, w_out, x2, g2, seq)

    return _final_norm(x2, final_g.reshape(1, d)).reshape(batch, seq, d)
```

```python
import functools

import jax
import jax.numpy as jnp
from jax import lax
from jax.experimental import pallas as pl
from jax.experimental.pallas import tpu as pltpu

HEAD_DIM = 128
CHUNK = 128
GM_GROUP = 128
CONV_W = 3
EPS = 1e-6
LOG2E = 1.4426950408889634
LANES = 128
SUBLANES = 8
VMEM_LIMIT_BYTES = 56 * 1024 * 1024
RESIDENT_WEIGHT_BYTES = 16 * 1024 * 1024
MAX_LOGIT_EXCESS = 60.0
FFN_ROW_CHUNK = 256
FFN_COL_TILE = 512

F32 = jnp.float32
BF16 = jnp.bfloat16


def _pick(n, prefs):
    for p in prefs:
        if n % p == 0:
            return p
    return n


def _round_up(n, m):
    return (n + m - 1) // m * m


def _col_tiles(w, tn):
    k, n = w.shape
    return w.reshape(k, n // tn, tn).transpose(1, 0, 2).reshape(n // tn * k, tn)


def _params(*sem):
    return pltpu.CompilerParams(dimension_semantics=sem, vmem_limit_bytes=VMEM_LIMIT_BYTES)


def _silu(x):
    return x / (1.0 + jnp.exp(-x))


def _mod_kernel(c_ref, w_ref, b_ref, o_ref):
    ca = _silu(c_ref[...]).astype(BF16)
    o_ref[0] = jnp.dot(ca, w_ref[0].astype(BF16), preferred_element_type=F32) + b_ref[0]


def _modulation(c, mod_w, mod_b):
    depth, d, n = mod_w.shape
    b = c.shape[0]
    tn = _pick(n, (1024, 512, 256, 128))
    return pl.pallas_call(
        _mod_kernel,
        grid=(depth, n // tn),
        in_specs=[
            pl.BlockSpec((b, d), lambda l, j: (0, 0)),
            pl.BlockSpec((1, d, tn), lambda l, j: (l, 0, j)),
            pl.BlockSpec((1, 1, tn), lambda l, j: (l, 0, j)),
        ],
        out_specs=pl.BlockSpec((1, b, tn), lambda l, j: (l, 0, j)),
        out_shape=jax.ShapeDtypeStruct((depth, b, n), F32),
        compiler_params=_params("arbitrary", "arbitrary"),
        name="adaln_modulation",
    )(c, mod_w, mod_b.reshape(depth, 1, n))


def _norm_modulate(x_ref, g_ref, sc_ref, sh_ref, h_ref):
    x = x_ref[...]
    r = lax.rsqrt(jnp.mean(x * x, axis=-1, keepdims=True) + EPS)
    a = g_ref[...] * (1.0 + sc_ref[0])
    h_ref[...] = ((x * r) * a + sh_ref[0]).astype(BF16)


def _row_specs(tm, d, rows_per_batch):
    tiles_per_batch = rows_per_batch // tm
    return [
        pl.BlockSpec((tm, d), lambda i, j: (i, 0)),
        pl.BlockSpec((1, d), lambda i, j: (0, 0)),
        pl.BlockSpec((1, 1, d), lambda i, j: (i // tiles_per_batch, 0, 0)),
        pl.BlockSpec((1, 1, d), lambda i, j: (i // tiles_per_batch, 0, 0)),
    ]


def _attn_in_kernel(x_ref, g_ref, sc_ref, sh_ref, w_ref, wf_ref, bf_ref, qkv_ref, fl_ref, h_ref,
                    *, q_tiles, q_scale):
    j = pl.program_id(1)

    @pl.when(j == 0)
    def _():
        _norm_modulate(x_ref, g_ref, sc_ref, sh_ref, h_ref)
        fl_ref[...] = jnp.dot(h_ref[...], wf_ref[...], preferred_element_type=F32) + bf_ref[...]

    acc = jnp.dot(h_ref[...], w_ref[...], preferred_element_type=F32)
    acc = acc * jnp.where(j < q_tiles, q_scale, 1.0)
    qkv_ref[...] = acc.astype(BF16)


def _attn_in_proj(x2, g, sc, sh, w_qkv, w_f, b_f, rows_per_batch):
    m, d = x2.shape
    n = w_qkv.shape[1]
    tm = _pick(rows_per_batch, (1024, 512, 256, 128))
    tn = _pick(d, (1024, 512, 256, 128))
    kern = functools.partial(_attn_in_kernel, q_tiles=d // tn, q_scale=HEAD_DIM ** -0.5 * LOG2E)
    return pl.pallas_call(
        kern,
        grid=(m // tm, n // tn),
        in_specs=_row_specs(tm, d, rows_per_batch) + [
            pl.BlockSpec((d, tn), lambda i, j: (j, 0)),
            pl.BlockSpec((d, LANES), lambda i, j: (0, 0)),
            pl.BlockSpec((1, LANES), lambda i, j: (0, 0)),
        ],
        out_specs=[
            pl.BlockSpec((tm, tn), lambda i, j: (i, j)),
            pl.BlockSpec((tm, LANES), lambda i, j: (i, 0)),
        ],
        out_shape=[
            jax.ShapeDtypeStruct((m, n), BF16),
            jax.ShapeDtypeStruct((m, LANES), F32),
        ],
        scratch_shapes=[pltpu.VMEM((tm, d), BF16)],
        compiler_params=_params("arbitrary", "arbitrary"),
        name="attn_in_proj",
    )(x2, g, sc, sh, _col_tiles(w_qkv, tn), w_f, b_f)


def _split3(x):
    hi = x.astype(BF16)
    r1 = x - hi.astype(F32)
    mid = r1.astype(BF16)
    lo = (r1 - mid.astype(F32)).astype(BF16)
    return hi, mid, lo


def _forget_cumsum_kernel(fl_ref, f_ref, ft_ref, carry_ref, *, sub):
    @pl.when(pl.program_id(1) == 0)
    def _():
        carry_ref[...] = jnp.zeros_like(carry_ref)

    tc = fl_ref.shape[0]
    row = lax.broadcasted_iota(jnp.int32, (sub, sub), 0)
    col = lax.broadcasted_iota(jnp.int32, (sub, sub), 1)
    tril = jnp.where(row >= col, 1.0, 0.0).astype(BF16)
    carry = carry_ref[0:1, :]
    for r in range(tc // sub):
        z = fl_ref[r * sub:(r + 1) * sub, :]
        lf = jnp.minimum(z, 0.0) - jnp.log1p(jnp.exp(-jnp.abs(z)))
        hi, mid, lo = _split3(lf)
        c = (jnp.dot(tril, lo, preferred_element_type=F32)
             + jnp.dot(tril, mid, preferred_element_type=F32)
             + jnp.dot(tril, hi, preferred_element_type=F32)) + carry
        c2 = c * LOG2E
        f_ref[r * sub:(r + 1) * sub, :] = c2
        ft_ref[0, :, r * sub:(r + 1) * sub] = c2.T
        carry = c[sub - 1:sub, :]
    carry_ref[0:1, :] = carry


def _forget_cumsum(flog, batch, seq):
    tc = _pick(seq, (1024, 512, 256, 128))
    sub = _pick(tc, (256, 128))
    nt = seq // tc
    return pl.pallas_call(
        functools.partial(_forget_cumsum_kernel, sub=sub),
        grid=(batch, nt),
        in_specs=[pl.BlockSpec((tc, LANES), lambda b, t: (b * nt + t, 0))],
        out_specs=[
            pl.BlockSpec((tc, LANES), lambda b, t: (b * nt + t, 0)),
            pl.BlockSpec((1, LANES, tc), lambda b, t: (b, 0, t)),
        ],
        out_shape=[
            jax.ShapeDtypeStruct((batch * seq, LANES), F32),
            jax.ShapeDtypeStruct((batch, LANES, seq), F32),
        ],
        scratch_shapes=[pltpu.VMEM((SUBLANES, LANES), F32)],
        compiler_params=_params("arbitrary", "arbitrary"),
        name="forget_cumsum",
    )(flog)


def _fox_attn_kernel(q_ref, k_ref, v_ref, f_ref, ft_ref, o_ref, vt_ref, fkb_ref, m_ref, l_ref, acc_ref, exc_ref,
                     *, t, hg):
    qi = pl.program_id(2)
    nq = vt_ref.shape[1]
    heads = [pl.program_id(1) * hg + g for g in range(hg)]

    @pl.when(qi == 0)
    def _():
        lane = lax.broadcasted_iota(jnp.int32, (t, LANES), 1)
        for g in range(hg):
            for c in range(nq):
                rows = slice(c * t, (c + 1) * t)
                vt_ref[g, c] = v_ref[rows, g * HEAD_DIM:(g + 1) * HEAD_DIM].T
                fk = jnp.sum(jnp.where(lane == heads[g], f_ref[rows, :], 0.0), axis=-1, keepdims=True)
                fkb_ref[g, rows, :] = jnp.broadcast_to(fk, (t, LANES))

    qs = [q_ref[:, g * HEAD_DIM:(g + 1) * HEAD_DIM] for g in range(hg)]
    fqs = [ft_ref[0, heads[g], pl.ds(qi, 1), :] for g in range(hg)]

    def scores(g, j):
        start = pl.multiple_of(j * t, t)
        kj = k_ref[pl.ds(start, t), g * HEAD_DIM:(g + 1) * HEAD_DIM]
        s = lax.dot_general(kj, qs[g], (((1,), (1,)), ((), ())), preferred_element_type=F32)
        return s - jnp.tile(fkb_ref[g, pl.ds(start, t), :], (1, t // LANES))

    def update(g, j, s, m_prev, l_prev, acc_prev):
        m_new = jnp.maximum(m_prev, jnp.max(s, axis=0, keepdims=True) + fqs[g])
        alpha = jnp.exp2(m_prev - m_new)
        p = jnp.exp2(s - (m_new - fqs[g]))
        l_new = alpha * l_prev + jnp.sum(p, axis=0, keepdims=True)
        pv = jnp.dot(vt_ref[g, j], p.astype(BF16), preferred_element_type=F32)
        return m_new, l_new, alpha * acc_prev + pv

    def update_from_running_max(g, j, s, m_prev, l_prev, acc_prev):
        p = jnp.exp2(s - (m_prev - fqs[g]))
        bmax = jnp.max(s, axis=0, keepdims=True) + fqs[g]
        l_mid = l_prev + jnp.sum(p, axis=0, keepdims=True)
        acc_mid = acc_prev + jnp.dot(vt_ref[g, j], p.astype(BF16), preferred_element_type=F32)
        m_new = jnp.maximum(m_prev, bmax)
        alpha = jnp.exp2(m_prev - m_new)
        return (m_new, alpha * l_mid, alpha * acc_mid), bmax - m_prev

    def load_state(g):
        return m_ref[g], l_ref[g], acc_ref[g]

    def store_state(g, state):
        m_ref[g], l_ref[g], acc_ref[g] = state

    def exact_body(it, _):
        j = qi - 1 - it
        for g in range(hg):
            store_state(g, update(g, j, scores(g, j), *load_state(g)))
        return 0

    def fast_body(it, _):
        j = qi - 1 - it
        excess = exc_ref[...]
        s_next = scores(0, j)
        for g in range(hg):
            s_cur = s_next
            if g + 1 < hg:
                s_next = scores(g + 1, j)
            new_g, exc_g = update_from_running_max(g, j, s_cur, *load_state(g))
            store_state(g, new_g)
            excess = jnp.maximum(excess, exc_g)
        exc_ref[...] = excess
        return 0

    half = t // 2
    causal = (lax.broadcasted_iota(jnp.int32, (half, half), 0) <= lax.broadcasted_iota(jnp.int32, (half, half), 1))

    def diag_scores(g):
        start = pl.multiple_of(qi * t, t)
        cols = slice(g * HEAD_DIM, (g + 1) * HEAD_DIM)
        nt = (((1,), (1,)), ((), ()))
        sa = lax.dot_general(k_ref[pl.ds(start, half), cols], qs[g], nt, preferred_element_type=F32)
        sa = sa - jnp.tile(fkb_ref[g, pl.ds(start, half), :], (1, t // LANES))
        sb = lax.dot_general(k_ref[pl.ds(start + half, half), cols], qs[g][half:, :], nt,
                             preferred_element_type=F32)
        sb = sb - jnp.tile(fkb_ref[g, pl.ds(start + half, half), :], (1, half // LANES))
        sa = jnp.concatenate([jnp.where(causal, sa[:, :half], -jnp.inf), sa[:, half:]], axis=1)
        return sa, jnp.where(causal, sb, -jnp.inf)

    def late(x_all, x_late, op):
        return jnp.concatenate([x_all[:, :half], op(x_all[:, half:], x_late)], axis=1)

    def diag_update(g, sa, sb):
        m_new = late(jnp.max(sa, axis=0, keepdims=True), jnp.max(sb, axis=0, keepdims=True), jnp.maximum) + fqs[g]
        shift = m_new - fqs[g]
        pa = jnp.exp2(sa - shift)
        pb = jnp.exp2(sb - shift[:, half:])
        l_new = late(jnp.sum(pa, axis=0, keepdims=True), jnp.sum(pb, axis=0, keepdims=True), jnp.add)
        vt = vt_ref[g, qi]
        acc_a = jnp.dot(vt[:, :half], pa.astype(BF16), preferred_element_type=F32)
        acc_b = jnp.dot(vt[:, half:], pb.astype(BF16), preferred_element_type=F32)
        return m_new, l_new, late(acc_a, acc_b, jnp.add)

    def diag_block():
        s_diag = [diag_scores(g) for g in range(hg)]
        for g in range(hg):
            store_state(g, diag_update(g, *s_diag[g]))

    diag_block()
    exc_ref[...] = jnp.zeros_like(exc_ref)
    lax.fori_loop(0, qi, fast_body, 0)

    @pl.when(jnp.max(exc_ref[...]) > MAX_LOGIT_EXCESS)
    def _():
        diag_block()
        lax.fori_loop(0, qi, exact_body, 0)

    for g in range(hg):
        o_ref[:, g * HEAD_DIM:(g + 1) * HEAD_DIM] = (acc_ref[g] / l_ref[g]).T.astype(BF16)


def _fox_attention(qkv, f, ft4, batch, seq, d):
    heads = d // HEAD_DIM
    hg = _pick(heads, (4, 2, 1))
    wg = hg * HEAD_DIM
    ng = heads // hg
    t = ft4.shape[-1]
    assert t % (2 * LANES) == 0, "the diagonal block is processed in lane-aligned halves"
    nq = seq // t
    return pl.pallas_call(
        functools.partial(_fox_attn_kernel, t=t, hg=hg),
        grid=(batch, ng, nq),
        in_specs=[
            pl.BlockSpec((t, wg), lambda b, h, i: (b * nq + i, h)),
            pl.BlockSpec((seq, wg), lambda b, h, i: (b, ng + h)),
            pl.BlockSpec((seq, wg), lambda b, h, i: (b, 2 * ng + h)),
            pl.BlockSpec((seq, LANES), lambda b, h, i: (b, 0)),
            pl.BlockSpec((1, heads, nq, t), lambda b, h, i: (b, 0, 0, 0)),
        ],
        out_specs=pl.BlockSpec((t, wg), lambda b, h, i: (b * nq + i, h)),
        out_shape=jax.ShapeDtypeStruct((batch * seq, d), BF16),
        scratch_shapes=[
            pltpu.VMEM((hg, nq, HEAD_DIM, t), BF16),
            pltpu.VMEM((hg, seq, LANES), F32),
            pltpu.VMEM((hg, 1, t), F32),
            pltpu.VMEM((hg, 1, t), F32),
            pltpu.VMEM((hg, HEAD_DIM, t), F32),
            pltpu.VMEM((1, t), F32),
        ],
        compiler_params=_params("arbitrary", "arbitrary", "arbitrary"),
        name="fox_attention",
    )(qkv, qkv, qkv, f, ft4)


def _gelu_tanh(x):
    c = 0.7978845608028654
    return 0.5 * x * (1.0 + jnp.tanh(c * (x + 0.044715 * (x * x * x))))


def _gm_in_kernel(x_ref, g_ref, sc_ref, sh_ref, w_ref, z_ref, h_ref):
    @pl.when(pl.program_id(1) == 0)
    def _():
        _norm_modulate(x_ref, g_ref, sc_ref, sh_ref, h_ref)

    acc = jnp.dot(h_ref[...], w_ref[...], preferred_element_type=F32)
    z_ref[...] = _gelu_tanh(acc).astype(BF16)


def _gm_in_proj(x2, g, sc, sh, w, rows_per_batch):
    m, d = x2.shape
    n = w.shape[1]
    tm = _pick(rows_per_batch, (1024, 512, 256, 128))
    tn = _pick(n, (1024, 512, 256, 128))
    return pl.pallas_call(
        _gm_in_kernel,
        grid=(m // tm, n // tn),
        in_specs=_row_specs(tm, d, rows_per_batch) + [pl.BlockSpec((d, tn), lambda i, j: (j, 0))],
        out_specs=pl.BlockSpec((tm, tn), lambda i, j: (i, j)),
        out_shape=jax.ShapeDtypeStruct((m, n), BF16),
        scratch_shapes=[pltpu.VMEM((tm, d), BF16)],
        compiler_params=_params("arbitrary", "arbitrary"),
        name="gmlp_in_proj",
    )(x2, g, sc, sh, _col_tiles(w, tn))


def _gm_gate_kernel(u_ref, v_ref, vg_ref, ws_ref, bst_ref, o_ref):
    rows, dg = v_ref.shape
    groups = dg // GM_GROUP
    v = v_ref[...].astype(F32)
    r = lax.rsqrt(jnp.mean(v * v, axis=-1, keepdims=True) + EPS)
    vn = ((v * r) * vg_ref[...]).astype(BF16)
    row = lax.broadcasted_iota(jnp.int32, (CHUNK, CHUNK), 0)
    col = lax.broadcasted_iota(jnp.int32, (CHUNK, CHUNK), 1)
    causal = col <= row
    for g in range(groups):
        w = jnp.where(causal, ws_ref[g], 0.0).astype(BF16)
        bias = bst_ref[:, g:g + 1]
        lo = g * GM_GROUP
        for c in range(rows // CHUNK):
            r0 = c * CHUNK
            sv = jnp.dot(w, vn[r0:r0 + CHUNK, lo:lo + GM_GROUP], preferred_element_type=F32) + bias
            u = u_ref[r0:r0 + CHUNK, lo:lo + GM_GROUP].astype(F32)
            o_ref[r0:r0 + CHUNK, lo:lo + GM_GROUP] = (u * sv).astype(BF16)


def _gm_gate(z, v_g, w_s, b_s_t, rows_per_batch):
    m, n2 = z.shape
    dg = n2 // 2
    groups = dg // GM_GROUP
    tr = _pick(rows_per_batch, (512, 256, 128))
    return pl.pallas_call(
        _gm_gate_kernel,
        grid=(m // tr,),
        in_specs=[
            pl.BlockSpec((tr, dg), lambda i: (i, 0)),
            pl.BlockSpec((tr, dg), lambda i: (i, 1)),
            pl.BlockSpec((1, dg), lambda i: (0, 0)),
            pl.BlockSpec((groups, CHUNK, CHUNK), lambda i: (0, 0, 0)),
            pl.BlockSpec((CHUNK, groups), lambda i: (0, 0)),
        ],
        out_specs=pl.BlockSpec((tr, dg), lambda i: (i, 0)),
        out_shape=jax.ShapeDtypeStruct((m, dg), BF16),
        compiler_params=_params("arbitrary"),
        name="gmlp_spatial_gate",
    )(z, z, v_g, w_s, b_s_t)


def _ffn_in_kernel(x_ref, g_ref, sc_ref, sh_ref, w_ref, cw_ref, cb_ref, o_ref, h_ref, carry_ref,
                   *, tiles_per_batch):
    i = pl.program_id(0)
    j = pl.program_id(1)
    tm, d = x_ref.shape
    tn = o_ref.shape[1]
    halo = SUBLANES
    rc = min(tm, FFN_ROW_CHUNK)

    @pl.when(j == 0)
    def _():
        _norm_modulate(x_ref, g_ref, sc_ref, sh_ref, h_ref)

    @pl.when((i % tiles_per_batch) == 0)
    def _():
        carry_ref[j] = jnp.zeros(carry_ref.shape[1:], F32)

    sub = lax.broadcasted_iota(jnp.int32, (halo, tn), 0)
    halves = (slice(0, tn), slice(tn, 2 * tn))

    def shifted(acc, tail, k):
        moved = pltpu.roll(acc, k, 0)
        head = jnp.where(sub < k, pltpu.roll(tail, k, 0), moved[0:halo, :])
        return jnp.concatenate([head, moved[halo:, :]], axis=0)

    def conv(acc, tail, cols):
        return (cw_ref[0, 0:1, cols] * shifted(acc, tail, 2) + cw_ref[0, 1:2, cols] * shifted(acc, tail, 1)
                + cw_ref[0, 2:3, cols] * acc + cb_ref[0, :, cols])

    tails = [carry_ref[j, :, cols] for cols in halves]
    sizes = [rc] * (tm // rc)
    if rc // 2 >= 128 and rc % 16 == 0:
        sizes = sizes[:-1] + [rc // 2, rc // 2]
    r0 = 0
    for rc in sizes:
        h = h_ref[r0:r0 + rc, :]
        accs = [jnp.dot(h, w_ref[part * d:(part + 1) * d, :], preferred_element_type=F32) for part in range(2)]
        gate, up = (conv(acc, tail, cols) for acc, tail, cols in zip(accs, tails, halves))
        o_ref[r0:r0 + rc, :] = (_silu(gate) * up).astype(BF16)
        tails = [acc[rc - halo:rc, :] for acc in accs]
        r0 += rc

    for tail, cols in zip(tails, halves):
        carry_ref[j, :, cols] = tail


def _ffn_in(x2, g, sc, sh, w, cw, cb, rows_per_batch):
    m, d = x2.shape
    nj, _, tn2 = cw.shape
    tn = tn2 // 2
    tm = _pick(rows_per_batch, (1024, 512, 256, 128))
    kern = functools.partial(_ffn_in_kernel, tiles_per_batch=rows_per_batch // tm)
    col = lambda i, j: (j, 0, 0)
    return pl.pallas_call(
        kern,
        grid=(m // tm, nj),
        in_specs=_row_specs(tm, d, rows_per_batch) + [
            pl.BlockSpec((2 * d, tn), lambda i, j: (j, 0)),
            pl.BlockSpec((1, CONV_W, tn2), col),
            pl.BlockSpec((1, 1, tn2), col),
        ],
        out_specs=pl.BlockSpec((tm, tn), lambda i, j: (i, j)),
        out_shape=jax.ShapeDtypeStruct((m, nj * tn), BF16),
        scratch_shapes=[
            pltpu.VMEM((tm, d), BF16),
            pltpu.VMEM((nj, SUBLANES, tn2), F32),
        ],
        compiler_params=_params("arbitrary", "arbitrary"),
        name="ffn_in_conv_gate",
    )(x2, g, sc, sh, w, cw, cb)


def _out_proj_kernel(a_ref, w_ref, x_ref, gate_ref, o_ref):
    y = jnp.dot(a_ref[...], w_ref[...], preferred_element_type=F32)
    o_ref[...] = x_ref[...] + gate_ref[0] * y


def _out_proj_residual(a, w, x2, gate, rows_per_batch):
    m, k = a.shape
    n = w.shape[1]
    tm = _pick(rows_per_batch, (1024, 512, 256, 128))
    tn = _pick(n, (1024, 512, 256, 128))
    if k > 2 * n:
        tn = _pick(n, (512, 256, 128))
    elif 2 * k * n * 2 <= RESIDENT_WEIGHT_BYTES:
        tn = n
        tm = _pick(rows_per_batch, (512, 256, 128))
    tiles_per_batch = rows_per_batch // tm
    return pl.pallas_call(
        _out_proj_kernel,
        grid=(m // tm, n // tn),
        in_specs=[
            pl.BlockSpec((tm, k), lambda i, j: (i, 0)),
            pl.BlockSpec((k, tn), lambda i, j: (j, 0)),
            pl.BlockSpec((tm, tn), lambda i, j: (i, j)),
            pl.BlockSpec((1, 1, tn), lambda i, j: (i // tiles_per_batch, 0, j)),
        ],
        out_specs=pl.BlockSpec((tm, tn), lambda i, j: (i, j)),
        out_shape=jax.ShapeDtypeStruct((m, n), F32),
        compiler_params=_params("arbitrary", "arbitrary"),
        name="out_proj_residual",
    )(a, _col_tiles(w, tn), x2, gate)


def _final_norm_kernel(x_ref, g_ref, o_ref):
    x = x_ref[...]
    r = lax.rsqrt(jnp.mean(x * x, axis=-1, keepdims=True) + EPS)
    o_ref[...] = (x * r) * g_ref[...]


def _final_norm(x2, g):
    m, d = x2.shape
    tm = _pick(m, (1024, 512, 256, 128))
    return pl.pallas_call(
        _final_norm_kernel,
        grid=(m // tm,),
        in_specs=[pl.BlockSpec((tm, d), lambda i: (i, 0)), pl.BlockSpec((1, d), lambda i: (0, 0))],
        out_specs=pl.BlockSpec((tm, d), lambda i: (i, 0)),
        out_shape=jax.ShapeDtypeStruct((m, d), F32),
        compiler_params=_params("arbitrary"),
        name="final_rms_norm",
    )(x2, g)


def _pad_cols(a, n):
    return jnp.pad(a, ((0, 0), (0, n - a.shape[1])))


def _gate_up_tiles(a, d_ff, fp, stack_rows=False):
    r = a.shape[0]
    tn = FFN_COL_TILE
    gate = _pad_cols(a[:, :d_ff], fp).reshape(r, fp // tn, tn).transpose(1, 0, 2)
    up = _pad_cols(a[:, d_ff:], fp).reshape(r, fp // tn, tn).transpose(1, 0, 2)
    if stack_rows:
        return jnp.concatenate([gate, up], axis=1).reshape(-1, tn)
    return jnp.concatenate([gate, up], axis=2)


def kernel(x, c, mod_w, mod_b, mix_norm_g, ffn_norm_g, attn_w_in, attn_b_f, attn_w_o, gm_w_in, gm_v_g, gm_w_s,
           gm_b_s, gm_w_o, ffn_w_in, ffn_conv_w, ffn_conv_b, ffn_w_out, final_g):
    batch, seq, d = x.shape
    depth = mod_w.shape[0]
    heads = d // HEAD_DIM
    d_ff = ffn_w_in.shape[-1] // 2
    fp = _round_up(d_ff, FFN_COL_TILE)
    m = batch * seq
    assert d % LANES == 0 and seq % CHUNK == 0 and heads <= LANES
    t_attn = _pick(seq, (512, 256, 128))

    x2 = x.reshape(m, d)
    mod = _modulation(c, mod_w, mod_b).reshape(depth, batch, 6, 1, d)

    for i in range(depth):
        sh1, sc1, g1, sh2, sc2, g2 = (mod[i, :, k] for k in range(6))
        j = i // 2
        gm = mix_norm_g[i].reshape(1, d)
        if i % 2 == 0:
            w_in = attn_w_in[j]
            w_qkv = w_in[:, :3 * d].astype(BF16)
            w_f = _pad_cols(w_in[:, 3 * d:], LANES).astype(BF16)
            b_f = _pad_cols(attn_b_f[j].reshape(1, heads), LANES)
            qkv, flog = _attn_in_proj(x2, gm, sc1, sh1, w_qkv, w_f, b_f, seq)
            f, ft = _forget_cumsum(flog, batch, seq)
            ft4 = ft[:, :heads, :].reshape(batch, heads, seq // t_attn, t_attn)
            y = _fox_attention(qkv, f, ft4, batch, seq, d)
            w_o = attn_w_o[j].astype(BF16)
        else:
            z = _gm_in_proj(x2, gm, sc1, sh1, gm_w_in[j].astype(BF16), seq)
            y = _gm_gate(z, gm_v_g[j].reshape(1, -1), gm_w_s[j], gm_b_s[j].T, seq)
            w_o = gm_w_o[j].astype(BF16)
        x2 = _out_proj_residual(y, w_o, x2, g1, seq)

        a = _ffn_in(x2, ffn_norm_g[i].reshape(1, d), sc2, sh2,
                    _gate_up_tiles(ffn_w_in[i], d_ff, fp, stack_rows=True).astype(BF16),
                    _gate_up_tiles(ffn_conv_w[i], d_ff, fp),
                    _gate_up_tiles(ffn_conv_b[i].reshape(1, 2 * d_ff), d_ff, fp), seq)
        w_out = jnp.pad(ffn_w_out[i], ((0, fp - d_ff), (0, 0))).astype(BF16)
        x2 = _out_proj_residual(a, w_out, x2, g2, seq)

    return _final_norm(x2, final_g.reshape(1, d)).reshape(batch, seq, d)
```

```python
import functools

import jax
import jax.numpy as jnp
from jax import lax
from jax.experimental import pallas as pl
from jax.experimental.pallas import tpu as pltpu

HEAD_DIM = 128
CHUNK = 128
GM_GROUP = 128
CONV_W = 3
EPS = 1e-6
LOG2E = 1.4426950408889634
LANES = 128
SUBLANES = 8
VMEM_LIMIT_BYTES = 56 * 1024 * 1024
RESIDENT_WEIGHT_BYTES = 16 * 1024 * 1024
FAST_PATH_LIMIT = 1e30
FFN_ROW_CHUNK = 256

F32 = jnp.float32
BF16 = jnp.bfloat16


def _pick(n, prefs):
    for p in prefs:
        if n % p == 0:
            return p
    return n


def _round_up(n, m):
    return (n + m - 1) // m * m


def _params(*sem):
    return pltpu.CompilerParams(dimension_semantics=sem, vmem_limit_bytes=VMEM_LIMIT_BYTES)


def _silu(x):
    return x / (1.0 + jnp.exp(-x))


def _mod_kernel(c_ref, w_ref, b_ref, o_ref):
    ca = _silu(c_ref[...]).astype(BF16)
    o_ref[0] = jnp.dot(ca, w_ref[0].astype(BF16), preferred_element_type=F32) + b_ref[0]


def _modulation(c, mod_w, mod_b):
    depth, d, n = mod_w.shape
    b = c.shape[0]
    tn = _pick(n, (1024, 512, 256, 128))
    return pl.pallas_call(
        _mod_kernel,
        grid=(depth, n // tn),
        in_specs=[
            pl.BlockSpec((b, d), lambda l, j: (0, 0)),
            pl.BlockSpec((1, d, tn), lambda l, j: (l, 0, j)),
            pl.BlockSpec((1, 1, tn), lambda l, j: (l, 0, j)),
        ],
        out_specs=pl.BlockSpec((1, b, tn), lambda l, j: (l, 0, j)),
        out_shape=jax.ShapeDtypeStruct((depth, b, n), F32),
        compiler_params=_params("arbitrary", "arbitrary"),
        name="adaln_modulation",
    )(c, mod_w, mod_b.reshape(depth, 1, n))


def _norm_modulate(x_ref, g_ref, sc_ref, sh_ref, h_ref):
    x = x_ref[...]
    r = lax.rsqrt(jnp.mean(x * x, axis=-1, keepdims=True) + EPS)
    a = g_ref[...] * (1.0 + sc_ref[0])
    h_ref[...] = ((x * r) * a + sh_ref[0]).astype(BF16)


def _row_specs(tm, d, rows_per_batch):
    tiles_per_batch = rows_per_batch // tm
    return [
        pl.BlockSpec((tm, d), lambda i, j: (i, 0)),
        pl.BlockSpec((1, d), lambda i, j: (0, 0)),
        pl.BlockSpec((1, 1, d), lambda i, j: (i // tiles_per_batch, 0, 0)),
        pl.BlockSpec((1, 1, d), lambda i, j: (i // tiles_per_batch, 0, 0)),
    ]


def _attn_in_kernel(x_ref, g_ref, sc_ref, sh_ref, w_ref, wf_ref, bf_ref, qkv_ref, fl_ref, h_ref,
                    *, q_tiles, q_scale):
    j = pl.program_id(1)

    @pl.when(j == 0)
    def _():
        _norm_modulate(x_ref, g_ref, sc_ref, sh_ref, h_ref)
        fl_ref[...] = jnp.dot(h_ref[...], wf_ref[...], preferred_element_type=F32) + bf_ref[...]

    acc = jnp.dot(h_ref[...], w_ref[...], preferred_element_type=F32)
    acc = acc * jnp.where(j < q_tiles, q_scale, 1.0)
    qkv_ref[...] = acc.astype(BF16)


def _attn_in_proj(x2, g, sc, sh, w_qkv, w_f, b_f, rows_per_batch):
    m, d = x2.shape
    n = w_qkv.shape[1]
    tm = _pick(rows_per_batch, (1024, 512, 256, 128))
    tn = _pick(d, (1024, 512, 256, 128))
    kern = functools.partial(_attn_in_kernel, q_tiles=d // tn, q_scale=HEAD_DIM ** -0.5 * LOG2E)
    return pl.pallas_call(
        kern,
        grid=(m // tm, n // tn),
        in_specs=_row_specs(tm, d, rows_per_batch) + [
            pl.BlockSpec((d, tn), lambda i, j: (0, j)),
            pl.BlockSpec((d, LANES), lambda i, j: (0, 0)),
            pl.BlockSpec((1, LANES), lambda i, j: (0, 0)),
        ],
        out_specs=[
            pl.BlockSpec((tm, tn), lambda i, j: (i, j)),
            pl.BlockSpec((tm, LANES), lambda i, j: (i, 0)),
        ],
        out_shape=[
            jax.ShapeDtypeStruct((m, n), BF16),
            jax.ShapeDtypeStruct((m, LANES), F32),
        ],
        scratch_shapes=[pltpu.VMEM((tm, d), BF16)],
        compiler_params=_params("arbitrary", "arbitrary"),
        name="attn_in_proj",
    )(x2, g, sc, sh, w_qkv, w_f, b_f)


def _split3(x):
    hi = x.astype(BF16)
    r1 = x - hi.astype(F32)
    mid = r1.astype(BF16)
    lo = (r1 - mid.astype(F32)).astype(BF16)
    return hi, mid, lo


def _forget_cumsum_kernel(fl_ref, f_ref, ft_ref, carry_ref, *, sub):
    @pl.when(pl.program_id(1) == 0)
    def _():
        carry_ref[...] = jnp.zeros_like(carry_ref)

    tc = fl_ref.shape[0]
    row = lax.broadcasted_iota(jnp.int32, (sub, sub), 0)
    col = lax.broadcasted_iota(jnp.int32, (sub, sub), 1)
    tril = jnp.where(row >= col, 1.0, 0.0).astype(BF16)
    carry = carry_ref[0:1, :]
    for r in range(tc // sub):
        z = fl_ref[r * sub:(r + 1) * sub, :]
        lf = jnp.minimum(z, 0.0) - jnp.log1p(jnp.exp(-jnp.abs(z)))
        hi, mid, lo = _split3(lf)
        c = (jnp.dot(tril, lo, preferred_element_type=F32)
             + jnp.dot(tril, mid, preferred_element_type=F32)
             + jnp.dot(tril, hi, preferred_element_type=F32)) + carry
        c2 = c * LOG2E
        f_ref[r * sub:(r + 1) * sub, :] = c2
        ft_ref[0, :, r * sub:(r + 1) * sub] = c2.T
        carry = c[sub - 1:sub, :]
    carry_ref[0:1, :] = carry


def _forget_cumsum(flog, batch, seq):
    tc = _pick(seq, (1024, 512, 256, 128))
    sub = _pick(tc, (256, 128))
    nt = seq // tc
    return pl.pallas_call(
        functools.partial(_forget_cumsum_kernel, sub=sub),
        grid=(batch, nt),
        in_specs=[pl.BlockSpec((tc, LANES), lambda b, t: (b * nt + t, 0))],
        out_specs=[
            pl.BlockSpec((tc, LANES), lambda b, t: (b * nt + t, 0)),
            pl.BlockSpec((1, LANES, tc), lambda b, t: (b, 0, t)),
        ],
        out_shape=[
            jax.ShapeDtypeStruct((batch * seq, LANES), F32),
            jax.ShapeDtypeStruct((batch, LANES, seq), F32),
        ],
        scratch_shapes=[pltpu.VMEM((SUBLANES, LANES), F32)],
        compiler_params=_params("arbitrary", "arbitrary"),
        name="forget_cumsum",
    )(flog)


def _fox_attn_kernel(q_ref, k_ref, v_ref, f_ref, ft_ref, o_ref, vt_ref, fkb_ref, vmax_ref, m_ref, l_ref, acc_ref,
                     *, t, hg):
    qi = pl.program_id(2)
    nq = vt_ref.shape[1]
    heads = [pl.program_id(1) * hg + g for g in range(hg)]

    @pl.when(qi == 0)
    def _():
        lane = lax.broadcasted_iota(jnp.int32, (t, LANES), 1)
        for g in range(hg):
            v_abs = jnp.abs(v_ref[:, g * HEAD_DIM:(g + 1) * HEAD_DIM].astype(F32))
            vmax_ref[g] = jnp.max(v_abs, axis=0, keepdims=True)
            for c in range(nq):
                rows = slice(c * t, (c + 1) * t)
                vt_ref[g, c] = v_ref[rows, g * HEAD_DIM:(g + 1) * HEAD_DIM].T
                fk = jnp.sum(jnp.where(lane == heads[g], f_ref[rows, :], 0.0), axis=-1, keepdims=True)
                fkb_ref[g, rows, :] = jnp.broadcast_to(fk, (t, LANES))

    qs = [q_ref[:, g * HEAD_DIM:(g + 1) * HEAD_DIM] for g in range(hg)]
    fqs = [ft_ref[0, heads[g], pl.ds(qi, 1), :] for g in range(hg)]

    def scores(g, j):
        start = pl.multiple_of(j * t, t)
        kj = k_ref[pl.ds(start, t), g * HEAD_DIM:(g + 1) * HEAD_DIM]
        s = lax.dot_general(kj, qs[g], (((1,), (1,)), ((), ())), preferred_element_type=F32)
        return s - jnp.tile(fkb_ref[g, pl.ds(start, t), :], (1, t // LANES))

    def update(g, j, s, m_prev, l_prev, acc_prev):
        m_new = jnp.maximum(m_prev, jnp.max(s, axis=0, keepdims=True) + fqs[g])
        alpha = jnp.exp2(m_prev - m_new)
        p = jnp.exp2(s - (m_new - fqs[g]))
        l_new = alpha * l_prev + jnp.sum(p, axis=0, keepdims=True)
        pv = jnp.dot(vt_ref[g, j], p.astype(BF16), preferred_element_type=F32)
        return m_new, l_new, alpha * acc_prev + pv

    def load_state(g):
        return m_ref[g], l_ref[g], acc_ref[g]

    def store_state(g, state):
        m_ref[g], l_ref[g], acc_ref[g] = state

    def exact_body(it, _):
        j = qi - 1 - it
        for g in range(hg):
            store_state(g, update(g, j, scores(g, j), *load_state(g)))
        return 0

    def fast_body(it, _):
        j = qi - 1 - it
        s_next = scores(0, j)
        for g in range(hg):
            s_cur = s_next
            if g + 1 < hg:
                s_next = scores(g + 1, j)
            p = jnp.exp2(s_cur - (m_ref[g] - fqs[g]))
            l_ref[g] += jnp.sum(p, axis=0, keepdims=True)
            acc_ref[g] += jnp.dot(vt_ref[g, j], p.astype(BF16), preferred_element_type=F32)
        return 0

    half = t // 2
    causal = (lax.broadcasted_iota(jnp.int32, (half, half), 0) <= lax.broadcasted_iota(jnp.int32, (half, half), 1))

    def diag_scores(g):
        start = pl.multiple_of(qi * t, t)
        cols = slice(g * HEAD_DIM, (g + 1) * HEAD_DIM)
        nt = (((1,), (1,)), ((), ()))
        sa = lax.dot_general(k_ref[pl.ds(start, half), cols], qs[g], nt, preferred_element_type=F32)
        sa = sa - jnp.tile(fkb_ref[g, pl.ds(start, half), :], (1, t // LANES))
        sb = lax.dot_general(k_ref[pl.ds(start + half, half), cols], qs[g][half:, :], nt,
                             preferred_element_type=F32)
        sb = sb - jnp.tile(fkb_ref[g, pl.ds(start + half, half), :], (1, half // LANES))
        sa = jnp.concatenate([jnp.where(causal, sa[:, :half], -jnp.inf), sa[:, half:]], axis=1)
        return sa, jnp.where(causal, sb, -jnp.inf)

    def late(x_all, x_late, op):
        return jnp.concatenate([x_all[:, :half], op(x_all[:, half:], x_late)], axis=1)

    def diag_update(g, sa, sb):
        m_new = late(jnp.max(sa, axis=0, keepdims=True), jnp.max(sb, axis=0, keepdims=True), jnp.maximum) + fqs[g]
        shift = m_new - fqs[g]
        pa = jnp.exp2(sa - shift)
        pb = jnp.exp2(sb - shift[:, half:])
        l_new = late(jnp.sum(pa, axis=0, keepdims=True), jnp.sum(pb, axis=0, keepdims=True), jnp.add)
        vt = vt_ref[g, qi]
        acc_a = jnp.dot(vt[:, :half], pa.astype(BF16), preferred_element_type=F32)
        acc_b = jnp.dot(vt[:, half:], pb.astype(BF16), preferred_element_type=F32)
        return m_new, l_new, late(acc_a, acc_b, jnp.add)

    def diag_block():
        s_diag = [diag_scores(g) for g in range(hg)]
        for g in range(hg):
            store_state(g, diag_update(g, *s_diag[g]))

    diag_block()
    lax.fori_loop(0, qi, fast_body, 0)

    risk = jnp.zeros((1, t), F32)
    for g in range(hg):
        bound = l_ref[g] * jnp.maximum(jnp.max(vmax_ref[g], axis=-1, keepdims=True), 1.0)
        risk = jnp.maximum(risk, jnp.where(bound < FAST_PATH_LIMIT, 0.0, 1.0))

    @pl.when(jnp.max(risk) > 0.0)
    def _():
        diag_block()
        lax.fori_loop(0, qi, exact_body, 0)

    for g in range(hg):
        o_ref[:, g * HEAD_DIM:(g + 1) * HEAD_DIM] = (acc_ref[g] / l_ref[g]).T.astype(BF16)


def _fox_attention(qkv, f, ft4, batch, seq, d):
    heads = d // HEAD_DIM
    hg = _pick(heads, (4, 2, 1))
    wg = hg * HEAD_DIM
    ng = heads // hg
    t = ft4.shape[-1]
    assert t % (2 * LANES) == 0, "the diagonal block is processed in lane-aligned halves"
    nq = seq // t
    return pl.pallas_call(
        functools.partial(_fox_attn_kernel, t=t, hg=hg),
        grid=(batch, ng, nq),
        in_specs=[
            pl.BlockSpec((t, wg), lambda b, h, i: (b * nq + i, h)),
            pl.BlockSpec((seq, wg), lambda b, h, i: (b, ng + h)),
            pl.BlockSpec((seq, wg), lambda b, h, i: (b, 2 * ng + h)),
            pl.BlockSpec((seq, LANES), lambda b, h, i: (b, 0)),
            pl.BlockSpec((1, heads, nq, t), lambda b, h, i: (b, 0, 0, 0)),
        ],
        out_specs=pl.BlockSpec((t, wg), lambda b, h, i: (b * nq + i, h)),
        out_shape=jax.ShapeDtypeStruct((batch * seq, d), BF16),
        scratch_shapes=[
            pltpu.VMEM((hg, nq, HEAD_DIM, t), BF16),
            pltpu.VMEM((hg, seq, LANES), F32),
            pltpu.VMEM((hg, 1, HEAD_DIM), F32),
            pltpu.VMEM((hg, 1, t), F32),
            pltpu.VMEM((hg, 1, t), F32),
            pltpu.VMEM((hg, HEAD_DIM, t), F32),
        ],
        compiler_params=_params("arbitrary", "arbitrary", "arbitrary"),
        name="fox_attention",
    )(qkv, qkv, qkv, f, ft4)


def _gelu_tanh(x):
    c = 0.7978845608028654
    return 0.5 * x * (1.0 + jnp.tanh(c * (x + 0.044715 * (x * x * x))))


def _gm_in_kernel(x_ref, g_ref, sc_ref, sh_ref, w_ref, z_ref, h_ref):
    @pl.when(pl.program_id(1) == 0)
    def _():
        _norm_modulate(x_ref, g_ref, sc_ref, sh_ref, h_ref)

    acc = jnp.dot(h_ref[...], w_ref[...], preferred_element_type=F32)
    z_ref[...] = _gelu_tanh(acc).astype(BF16)


def _gm_in_proj(x2, g, sc, sh, w, rows_per_batch):
    m, d = x2.shape
    n = w.shape[1]
    tm = _pick(rows_per_batch, (1024, 512, 256, 128))
    tn = _pick(n, (1024, 512, 256, 128))
    return pl.pallas_call(
        _gm_in_kernel,
        grid=(m // tm, n // tn),
        in_specs=_row_specs(tm, d, rows_per_batch) + [pl.BlockSpec((d, tn), lambda i, j: (0, j))],
        out_specs=pl.BlockSpec((tm, tn), lambda i, j: (i, j)),
        out_shape=jax.ShapeDtypeStruct((m, n), BF16),
        scratch_shapes=[pltpu.VMEM((tm, d), BF16)],
        compiler_params=_params("arbitrary", "arbitrary"),
        name="gmlp_in_proj",
    )(x2, g, sc, sh, w)


def _gm_gate_kernel(u_ref, v_ref, vg_ref, ws_ref, bst_ref, o_ref):
    rows, dg = v_ref.shape
    groups = dg // GM_GROUP
    v = v_ref[...].astype(F32)
    r = lax.rsqrt(jnp.mean(v * v, axis=-1, keepdims=True) + EPS)
    vn = ((v * r) * vg_ref[...]).astype(BF16)
    row = lax.broadcasted_iota(jnp.int32, (CHUNK, CHUNK), 0)
    col = lax.broadcasted_iota(jnp.int32, (CHUNK, CHUNK), 1)
    causal = col <= row
    for g in range(groups):
        w = jnp.where(causal, ws_ref[g], 0.0).astype(BF16)
        bias = bst_ref[:, g:g + 1]
        lo = g * GM_GROUP
        for c in range(rows // CHUNK):
            r0 = c * CHUNK
            sv = jnp.dot(w, vn[r0:r0 + CHUNK, lo:lo + GM_GROUP], preferred_element_type=F32) + bias
            u = u_ref[r0:r0 + CHUNK, lo:lo + GM_GROUP].astype(F32)
            o_ref[r0:r0 + CHUNK, lo:lo + GM_GROUP] = (u * sv).astype(BF16)


def _gm_gate(z, v_g, w_s, b_s_t, rows_per_batch):
    m, n2 = z.shape
    dg = n2 // 2
    groups = dg // GM_GROUP
    tr = _pick(rows_per_batch, (512, 256, 128))
    return pl.pallas_call(
        _gm_gate_kernel,
        grid=(m // tr,),
        in_specs=[
            pl.BlockSpec((tr, dg), lambda i: (i, 0)),
            pl.BlockSpec((tr, dg), lambda i: (i, 1)),
            pl.BlockSpec((1, dg), lambda i: (0, 0)),
            pl.BlockSpec((groups, CHUNK, CHUNK), lambda i: (0, 0, 0)),
            pl.BlockSpec((CHUNK, groups), lambda i: (0, 0)),
        ],
        out_specs=pl.BlockSpec((tr, dg), lambda i: (i, 0)),
        out_shape=jax.ShapeDtypeStruct((m, dg), BF16),
        compiler_params=_params("arbitrary"),
        name="gmlp_spatial_gate",
    )(z, z, v_g, w_s, b_s_t)


def _ffn_in_kernel(x_ref, g_ref, sc_ref, sh_ref, wg_ref, wu_ref, cwg_ref, cwu_ref, cbg_ref, cbu_ref,
                   o_ref, h_ref, carry_g_ref, carry_u_ref, *, tiles_per_batch):
    i = pl.program_id(0)
    j = pl.program_id(1)
    tm = x_ref.shape[0]
    halo = SUBLANES
    rc = min(tm, FFN_ROW_CHUNK)

    @pl.when(j == 0)
    def _():
        _norm_modulate(x_ref, g_ref, sc_ref, sh_ref, h_ref)

    @pl.when((i % tiles_per_batch) == 0)
    def _():
        carry_g_ref[j] = jnp.zeros(carry_g_ref.shape[1:], F32)
        carry_u_ref[j] = jnp.zeros(carry_u_ref.shape[1:], F32)

    sub = lax.broadcasted_iota(jnp.int32, (halo, wg_ref.shape[1]), 0)

    def shifted(acc, tail, k):
        moved = pltpu.roll(acc, k, 0)
        head = jnp.where(sub < k, pltpu.roll(tail, k, 0), moved[0:halo, :])
        return jnp.concatenate([head, moved[halo:, :]], axis=0)

    def conv(acc, tail, cw_ref, cb_ref):
        return (cw_ref[0:1, :] * shifted(acc, tail, 2) + cw_ref[1:2, :] * shifted(acc, tail, 1)
                + cw_ref[2:3, :] * acc + cb_ref[...])

    tail_g = carry_g_ref[j]
    tail_u = carry_u_ref[j]
    sizes = [rc] * (tm // rc)
    if rc // 2 >= 128 and rc % 16 == 0:
        sizes = sizes[:-1] + [rc // 2, rc // 2]
    r0 = 0
    for rc in sizes:
        h = h_ref[r0:r0 + rc, :]
        acc_g = jnp.dot(h, wg_ref[...], preferred_element_type=F32)
        acc_u = jnp.dot(h, wu_ref[...], preferred_element_type=F32)
        gate = conv(acc_g, tail_g, cwg_ref, cbg_ref)
        up = conv(acc_u, tail_u, cwu_ref, cbu_ref)
        o_ref[r0:r0 + rc, :] = (_silu(gate) * up).astype(BF16)
        tail_g = acc_g[rc - halo:rc, :]
        tail_u = acc_u[rc - halo:rc, :]
        r0 += rc

    carry_g_ref[j] = tail_g
    carry_u_ref[j] = tail_u


def _ffn_in(x2, g, sc, sh, wg, wu, cwg, cwu, cbg, cbu, rows_per_batch):
    m, d = x2.shape
    fp = wg.shape[1]
    tm = _pick(rows_per_batch, (1024, 512, 256, 128))
    tn = _pick(fp, (512, 256, 128))
    nj = fp // tn
    kern = functools.partial(_ffn_in_kernel, tiles_per_batch=rows_per_batch // tm)
    col = lambda i, j: (0, j)
    return pl.pallas_call(
        kern,
        grid=(m // tm, nj),
        in_specs=_row_specs(tm, d, rows_per_batch) + [
            pl.BlockSpec((d, tn), col),
            pl.BlockSpec((d, tn), col),
            pl.BlockSpec((CONV_W, tn), col),
            pl.BlockSpec((CONV_W, tn), col),
            pl.BlockSpec((1, tn), col),
            pl.BlockSpec((1, tn), col),
        ],
        out_specs=pl.BlockSpec((tm, tn), lambda i, j: (i, j)),
        out_shape=jax.ShapeDtypeStruct((m, fp), BF16),
        scratch_shapes=[
            pltpu.VMEM((tm, d), BF16),
            pltpu.VMEM((nj, SUBLANES, tn), F32),
            pltpu.VMEM((nj, SUBLANES, tn), F32),
        ],
        compiler_params=_params("arbitrary", "arbitrary"),
        name="ffn_in_conv_gate",
    )(x2, g, sc, sh, wg, wu, cwg, cwu, cbg, cbu)


def _out_proj_kernel(a_ref, w_ref, x_ref, gate_ref, o_ref):
    y = jnp.dot(a_ref[...], w_ref[...], preferred_element_type=F32)
    o_ref[...] = x_ref[...] + gate_ref[0] * y


def _out_proj_residual(a, w, x2, gate, rows_per_batch):
    m, k = a.shape
    n = w.shape[1]
    tm = _pick(rows_per_batch, (1024, 512, 256, 128))
    tn = _pick(n, (1024, 512, 256, 128))
    if k > 2 * n:
        tn = _pick(n, (512, 256, 128))
    elif 2 * k * n * 2 <= RESIDENT_WEIGHT_BYTES:
        tn = n
        tm = _pick(rows_per_batch, (512, 256, 128))
    tiles_per_batch = rows_per_batch // tm
    return pl.pallas_call(
        _out_proj_kernel,
        grid=(m // tm, n // tn),
        in_specs=[
            pl.BlockSpec((tm, k), lambda i, j: (i, 0)),
            pl.BlockSpec((k, tn), lambda i, j: (0, j)),
            pl.BlockSpec((tm, tn), lambda i, j: (i, j)),
            pl.BlockSpec((1, 1, tn), lambda i, j: (i // tiles_per_batch, 0, j)),
        ],
        out_specs=pl.BlockSpec((tm, tn), lambda i, j: (i, j)),
        out_shape=jax.ShapeDtypeStruct((m, n), F32),
        compiler_params=_params("arbitrary", "arbitrary"),
        name="out_proj_residual",
    )(a, w, x2, gate)


def _final_norm_kernel(x_ref, g_ref, o_ref):
    x = x_ref[...]
    r = lax.rsqrt(jnp.mean(x * x, axis=-1, keepdims=True) + EPS)
    o_ref[...] = (x * r) * g_ref[...]


def _final_norm(x2, g):
    m, d = x2.shape
    tm = _pick(m, (1024, 512, 256, 128))
    return pl.pallas_call(
        _final_norm_kernel,
        grid=(m // tm,),
        in_specs=[pl.BlockSpec((tm, d), lambda i: (i, 0)), pl.BlockSpec((1, d), lambda i: (0, 0))],
        out_specs=pl.BlockSpec((tm, d), lambda i: (i, 0)),
        out_shape=jax.ShapeDtypeStruct((m, d), F32),
        compiler_params=_params("arbitrary"),
        name="final_rms_norm",
    )(x2, g)


def _pad_cols(a, n):
    return jnp.pad(a, ((0, 0), (0, n - a.shape[1])))


def kernel(x, c, mod_w, mod_b, mix_norm_g, ffn_norm_g, attn_w_in, attn_b_f, attn_w_o, gm_w_in, gm_v_g, gm_w_s,
           gm_b_s, gm_w_o, ffn_w_in, ffn_conv_w, ffn_conv_b, ffn_w_out, final_g):
    batch, seq, d = x.shape
    depth = mod_w.shape[0]
    heads = d // HEAD_DIM
    d_ff = ffn_w_in.shape[-1] // 2
    fp = _round_up(d_ff, 512)
    m = batch * seq
    assert d % LANES == 0 and seq % CHUNK == 0 and heads <= LANES
    t_attn = _pick(seq, (512, 256, 128))

    x2 = x.reshape(m, d)
    mod = _modulation(c, mod_w, mod_b).reshape(depth, batch, 6, 1, d)

    for i in range(depth):
        sh1, sc1, g1, sh2, sc2, g2 = (mod[i, :, k] for k in range(6))
        j = i // 2
        gm = mix_norm_g[i].reshape(1, d)
        if i % 2 == 0:
            w_in = attn_w_in[j]
            w_qkv = w_in[:, :3 * d].astype(BF16)
            w_f = _pad_cols(w_in[:, 3 * d:], LANES).astype(BF16)
            b_f = _pad_cols(attn_b_f[j].reshape(1, heads), LANES)
            qkv, flog = _attn_in_proj(x2, gm, sc1, sh1, w_qkv, w_f, b_f, seq)
            f, ft = _forget_cumsum(flog, batch, seq)
            ft4 = ft[:, :heads, :].reshape(batch, heads, seq // t_attn, t_attn)
            y = _fox_attention(qkv, f, ft4, batch, seq, d)
            w_o = attn_w_o[j].astype(BF16)
        else:
            z = _gm_in_proj(x2, gm, sc1, sh1, gm_w_in[j].astype(BF16), seq)
            y = _gm_gate(z, gm_v_g[j].reshape(1, -1), gm_w_s[j], gm_b_s[j].T, seq)
            w_o = gm_w_o[j].astype(BF16)
        x2 = _out_proj_residual(y, w_o, x2, g1, seq)

        w_in = ffn_w_in[i]
        wg = _pad_cols(w_in[:, :d_ff], fp).astype(BF16)
        wu = _pad_cols(w_in[:, d_ff:], fp).astype(BF16)
        cw = ffn_conv_w[i]
        cb = ffn_conv_b[i].reshape(1, 2 * d_ff)
        a = _ffn_in(x2, ffn_norm_g[i].reshape(1, d), sc2, sh2, wg, wu,
                    _pad_cols(cw[:, :d_ff], fp), _pad_cols(cw[:, d_ff:], fp),
                    _pad_cols(cb[:, :d_ff], fp), _pad_cols(cb[:, d_ff:], fp), seq)
        w_out = jnp.pad(ffn_w_out[i], ((0, fp - d_ff), (0, 0))).astype(BF16)
        x2 = _out_proj_residual(a, w_out, x2, g2, seq)

    return _final_norm(x2, final_g.reshape(1, d)).reshape(batch, seq, d)
```

```python
import functools

import jax
import jax.numpy as jnp
from jax import lax
from jax.experimental import pallas as pl
from jax.experimental.pallas import tpu as pltpu

HEAD_DIM = 128
CHUNK = 128
GM_GROUP = 128
CONV_W = 3
EPS = 1e-6
LOG2E = 1.4426950408889634
LANES = 128
SUBLANES = 8
VMEM_LIMIT_BYTES = 56 * 1024 * 1024
RESIDENT_WEIGHT_BYTES = 16 * 1024 * 1024
FAST_PATH_LIMIT = 1e30
FFN_ROW_CHUNK = 256

F32 = jnp.float32
BF16 = jnp.bfloat16


def _pick(n, prefs):
    for p in prefs:
        if n % p == 0:
            return p
    return n


def _round_up(n, m):
    return (n + m - 1) // m * m


def _params(*sem):
    return pltpu.CompilerParams(dimension_semantics=sem, vmem_limit_bytes=VMEM_LIMIT_BYTES)


def _silu(x):
    return x / (1.0 + jnp.exp(-x))


def _mod_kernel(c_ref, w_ref, b_ref, o_ref):
    ca = _silu(c_ref[...]).astype(BF16)
    o_ref[0] = jnp.dot(ca, w_ref[0].astype(BF16), preferred_element_type=F32) + b_ref[0]


def _modulation(c, mod_w, mod_b):
    depth, d, n = mod_w.shape
    b = c.shape[0]
    tn = _pick(n, (1024, 512, 256, 128))
    return pl.pallas_call(
        _mod_kernel,
        grid=(depth, n // tn),
        in_specs=[
            pl.BlockSpec((b, d), lambda l, j: (0, 0)),
            pl.BlockSpec((1, d, tn), lambda l, j: (l, 0, j)),
            pl.BlockSpec((1, 1, tn), lambda l, j: (l, 0, j)),
        ],
        out_specs=pl.BlockSpec((1, b, tn), lambda l, j: (l, 0, j)),
        out_shape=jax.ShapeDtypeStruct((depth, b, n), F32),
        compiler_params=_params("arbitrary", "arbitrary"),
        name="adaln_modulation",
    )(c, mod_w, mod_b.reshape(depth, 1, n))


def _norm_modulate(x_ref, g_ref, sc_ref, sh_ref, h_ref, rows=slice(None)):
    x = x_ref[rows, :]
    r = lax.rsqrt(jnp.mean(x * x, axis=-1, keepdims=True) + EPS)
    a = g_ref[...] * (1.0 + sc_ref[0])
    h_ref[rows, :] = ((x * r) * a + sh_ref[0]).astype(BF16)


def _row_chunks(tm):
    rc = min(tm, FFN_ROW_CHUNK)
    return [slice(r0, r0 + rc) for r0 in range(0, tm, rc)]


def _row_specs(tm, d, rows_per_batch):
    tiles_per_batch = rows_per_batch // tm
    return [
        pl.BlockSpec((tm, d), lambda i, j: (i, 0)),
        pl.BlockSpec((1, d), lambda i, j: (0, 0)),
        pl.BlockSpec((1, 1, d), lambda i, j: (i // tiles_per_batch, 0, 0)),
        pl.BlockSpec((1, 1, d), lambda i, j: (i // tiles_per_batch, 0, 0)),
    ]


def _attn_in_kernel(x_ref, g_ref, sc_ref, sh_ref, w_ref, wf_ref, bf_ref, qkv_ref, fl_ref, h_ref,
                    *, q_tiles, q_scale):
    j = pl.program_id(1)
    scale = jnp.where(j < q_tiles, q_scale, 1.0)

    @pl.when(j == 0)
    def _():
        for rows in _row_chunks(x_ref.shape[0]):
            _norm_modulate(x_ref, g_ref, sc_ref, sh_ref, h_ref, rows)
            h = h_ref[rows, :]
            fl_ref[rows, :] = jnp.dot(h, wf_ref[...], preferred_element_type=F32) + bf_ref[...]
            qkv_ref[rows, :] = (jnp.dot(h, w_ref[...], preferred_element_type=F32) * scale).astype(BF16)

    @pl.when(j != 0)
    def _():
        acc = jnp.dot(h_ref[...], w_ref[...], preferred_element_type=F32)
        qkv_ref[...] = (acc * scale).astype(BF16)


def _attn_in_proj(x2, g, sc, sh, w_qkv, w_f, b_f, rows_per_batch):
    m, d = x2.shape
    n = w_qkv.shape[1]
    tm = _pick(rows_per_batch, (1024, 512, 256, 128))
    tn = _pick(d, (1024, 512, 256, 128))
    kern = functools.partial(_attn_in_kernel, q_tiles=d // tn, q_scale=HEAD_DIM ** -0.5 * LOG2E)
    return pl.pallas_call(
        kern,
        grid=(m // tm, n // tn),
        in_specs=_row_specs(tm, d, rows_per_batch) + [
            pl.BlockSpec((d, tn), lambda i, j: (0, j)),
            pl.BlockSpec((d, LANES), lambda i, j: (0, 0)),
            pl.BlockSpec((1, LANES), lambda i, j: (0, 0)),
        ],
        out_specs=[
            pl.BlockSpec((tm, tn), lambda i, j: (i, j)),
            pl.BlockSpec((tm, LANES), lambda i, j: (i, 0)),
        ],
        out_shape=[
            jax.ShapeDtypeStruct((m, n), BF16),
            jax.ShapeDtypeStruct((m, LANES), F32),
        ],
        scratch_shapes=[pltpu.VMEM((tm, d), BF16)],
        compiler_params=_params("arbitrary", "arbitrary"),
        name="attn_in_proj",
    )(x2, g, sc, sh, w_qkv, w_f, b_f)


def _split3(x):
    hi = x.astype(BF16)
    r1 = x - hi.astype(F32)
    mid = r1.astype(BF16)
    lo = (r1 - mid.astype(F32)).astype(BF16)
    return hi, mid, lo


def _forget_cumsum_kernel(fl_ref, f_ref, ft_ref, carry_ref, *, sub):
    @pl.when(pl.program_id(1) == 0)
    def _():
        carry_ref[...] = jnp.zeros_like(carry_ref)

    tc = fl_ref.shape[0]
    row = lax.broadcasted_iota(jnp.int32, (sub, sub), 0)
    col = lax.broadcasted_iota(jnp.int32, (sub, sub), 1)
    tril = jnp.where(row >= col, 1.0, 0.0).astype(BF16)
    carry = carry_ref[0:1, :]
    for r in range(tc // sub):
        z = fl_ref[r * sub:(r + 1) * sub, :]
        lf = jnp.minimum(z, 0.0) - jnp.log1p(jnp.exp(-jnp.abs(z)))
        hi, mid, lo = _split3(lf)
        c = (jnp.dot(tril, lo, preferred_element_type=F32)
             + jnp.dot(tril, mid, preferred_element_type=F32)
             + jnp.dot(tril, hi, preferred_element_type=F32)) + carry
        c2 = c * LOG2E
        f_ref[r * sub:(r + 1) * sub, :] = c2
        ft_ref[0, :, r * sub:(r + 1) * sub] = c2.T
        carry = c[sub - 1:sub, :]
    carry_ref[0:1, :] = carry


def _forget_cumsum(flog, batch, seq):
    tc = _pick(seq, (1024, 512, 256, 128))
    sub = _pick(tc, (256, 128))
    nt = seq // tc
    return pl.pallas_call(
        functools.partial(_forget_cumsum_kernel, sub=sub),
        grid=(batch, nt),
        in_specs=[pl.BlockSpec((tc, LANES), lambda b, t: (b * nt + t, 0))],
        out_specs=[
            pl.BlockSpec((tc, LANES), lambda b, t: (b * nt + t, 0)),
            pl.BlockSpec((1, LANES, tc), lambda b, t: (b, 0, t)),
        ],
        out_shape=[
            jax.ShapeDtypeStruct((batch * seq, LANES), F32),
            jax.ShapeDtypeStruct((batch, LANES, seq), F32),
        ],
        scratch_shapes=[pltpu.VMEM((SUBLANES, LANES), F32)],
        compiler_params=_params("arbitrary", "arbitrary"),
        name="forget_cumsum",
    )(flog)


def _fox_attn_kernel(q_ref, k_ref, v_ref, f_ref, ft_ref, o_ref, vt_ref, fkb_ref, vmax_ref, m_ref, l_ref, acc_ref,
                     *, t, hg):
    qi = pl.program_id(2)
    nq = vt_ref.shape[1]
    heads = [pl.program_id(1) * hg + g for g in range(hg)]

    @pl.when(qi == 0)
    def _():
        lane = lax.broadcasted_iota(jnp.int32, (t, LANES), 1)
        for g in range(hg):
            v_abs = jnp.abs(v_ref[:, g * HEAD_DIM:(g + 1) * HEAD_DIM].astype(F32))
            vmax_ref[g] = jnp.max(v_abs, axis=0, keepdims=True)
            for c in range(nq):
                rows = slice(c * t, (c + 1) * t)
                vt_ref[g, c] = v_ref[rows, g * HEAD_DIM:(g + 1) * HEAD_DIM].T
                fk = jnp.sum(jnp.where(lane == heads[g], f_ref[rows, :], 0.0), axis=-1, keepdims=True)
                fkb_ref[g, rows, :] = jnp.broadcast_to(fk, (t, LANES))

    qs = [q_ref[:, g * HEAD_DIM:(g + 1) * HEAD_DIM] for g in range(hg)]
    fqs = [ft_ref[0, heads[g], pl.ds(qi, 1), :] for g in range(hg)]

    def scores(g, j):
        start = pl.multiple_of(j * t, t)
        kj = k_ref[pl.ds(start, t), g * HEAD_DIM:(g + 1) * HEAD_DIM]
        s = lax.dot_general(kj, qs[g], (((1,), (1,)), ((), ())), preferred_element_type=F32)
        return s - jnp.tile(fkb_ref[g, pl.ds(start, t), :], (1, t // LANES))

    def update(g, j, s, m_prev, l_prev, acc_prev):
        m_new = jnp.maximum(m_prev, jnp.max(s, axis=0, keepdims=True) + fqs[g])
        alpha = jnp.exp2(m_prev - m_new)
        p = jnp.exp2(s - (m_new - fqs[g]))
        l_new = alpha * l_prev + jnp.sum(p, axis=0, keepdims=True)
        pv = jnp.dot(vt_ref[g, j], p.astype(BF16), preferred_element_type=F32)
        return m_new, l_new, alpha * acc_prev + pv

    def load_state(g):
        return m_ref[g], l_ref[g], acc_ref[g]

    def store_state(g, state):
        m_ref[g], l_ref[g], acc_ref[g] = state

    def exact_body(it, _):
        j = qi - 1 - it
        for g in range(hg):
            store_state(g, update(g, j, scores(g, j), *load_state(g)))
        return 0

    def fast_body(it, _):
        j = qi - 1 - it
        s_next = scores(0, j)
        for g in range(hg):
            s_cur = s_next
            if g + 1 < hg:
                s_next = scores(g + 1, j)
            p = jnp.exp2(s_cur - (m_ref[g] - fqs[g]))
            l_ref[g] += jnp.sum(p, axis=0, keepdims=True)
            acc_ref[g] += jnp.dot(vt_ref[g, j], p.astype(BF16), preferred_element_type=F32)
        return 0

    half = t // 2
    causal = (lax.broadcasted_iota(jnp.int32, (half, half), 0) <= lax.broadcasted_iota(jnp.int32, (half, half), 1))

    def diag_scores(g):
        start = pl.multiple_of(qi * t, t)
        cols = slice(g * HEAD_DIM, (g + 1) * HEAD_DIM)
        nt = (((1,), (1,)), ((), ()))
        sa = lax.dot_general(k_ref[pl.ds(start, half), cols], qs[g], nt, preferred_element_type=F32)
        sa = sa - jnp.tile(fkb_ref[g, pl.ds(start, half), :], (1, t // LANES))
        sb = lax.dot_general(k_ref[pl.ds(start + half, half), cols], qs[g][half:, :], nt,
                             preferred_element_type=F32)
        sb = sb - jnp.tile(fkb_ref[g, pl.ds(start + half, half), :], (1, half // LANES))
        sa = jnp.concatenate([jnp.where(causal, sa[:, :half], -jnp.inf), sa[:, half:]], axis=1)
        return sa, jnp.where(causal, sb, -jnp.inf)

    def late(x_all, x_late, op):
        return jnp.concatenate([x_all[:, :half], op(x_all[:, half:], x_late)], axis=1)

    def diag_update(g, sa, sb):
        m_new = late(jnp.max(sa, axis=0, keepdims=True), jnp.max(sb, axis=0, keepdims=True), jnp.maximum) + fqs[g]
        shift = m_new - fqs[g]
        pa = jnp.exp2(sa - shift)
        pb = jnp.exp2(sb - shift[:, half:])
        l_new = late(jnp.sum(pa, axis=0, keepdims=True), jnp.sum(pb, axis=0, keepdims=True), jnp.add)
        vt = vt_ref[g, qi]
        acc_a = jnp.dot(vt[:, :half], pa.astype(BF16), preferred_element_type=F32)
        acc_b = jnp.dot(vt[:, half:], pb.astype(BF16), preferred_element_type=F32)
        return m_new, l_new, late(acc_a, acc_b, jnp.add)

    def diag_block():
        s_diag = [diag_scores(g) for g in range(hg)]
        for g in range(hg):
            store_state(g, diag_update(g, *s_diag[g]))

    diag_block()
    lax.fori_loop(0, qi, fast_body, 0)

    risk = jnp.zeros((1, t), F32)
    for g in range(hg):
        bound = l_ref[g] * jnp.maximum(jnp.max(vmax_ref[g], axis=-1, keepdims=True), 1.0)
        risk = jnp.maximum(risk, jnp.where(bound < FAST_PATH_LIMIT, 0.0, 1.0))

    @pl.when(jnp.max(risk) > 0.0)
    def _():
        diag_block()
        lax.fori_loop(0, qi, exact_body, 0)

    for g in range(hg):
        o_ref[:, g * HEAD_DIM:(g + 1) * HEAD_DIM] = (acc_ref[g] / l_ref[g]).T.astype(BF16)


def _fox_attention(qkv, f, ft4, batch, seq, d):
    heads = d // HEAD_DIM
    hg = _pick(heads, (4, 2, 1))
    wg = hg * HEAD_DIM
    ng = heads // hg
    t = ft4.shape[-1]
    assert t % (2 * LANES) == 0, "the diagonal block is processed in lane-aligned halves"
    nq = seq // t
    return pl.pallas_call(
        functools.partial(_fox_attn_kernel, t=t, hg=hg),
        grid=(batch, ng, nq),
        in_specs=[
            pl.BlockSpec((t, wg), lambda b, h, i: (b * nq + i, h)),
            pl.BlockSpec((seq, wg), lambda b, h, i: (b, ng + h)),
            pl.BlockSpec((seq, wg), lambda b, h, i: (b, 2 * ng + h)),
            pl.BlockSpec((seq, LANES), lambda b, h, i: (b, 0)),
            pl.BlockSpec((1, heads, nq, t), lambda b, h, i: (b, 0, 0, 0)),
        ],
        out_specs=pl.BlockSpec((t, wg), lambda b, h, i: (b * nq + i, h)),
        out_shape=jax.ShapeDtypeStruct((batch * seq, d), BF16),
        scratch_shapes=[
            pltpu.VMEM((hg, nq, HEAD_DIM, t), BF16),
            pltpu.VMEM((hg, seq, LANES), F32),
            pltpu.VMEM((hg, 1, HEAD_DIM), F32),
            pltpu.VMEM((hg, 1, t), F32),
            pltpu.VMEM((hg, 1, t), F32),
            pltpu.VMEM((hg, HEAD_DIM, t), F32),
        ],
        compiler_params=_params("arbitrary", "arbitrary", "arbitrary"),
        name="fox_attention",
    )(qkv, qkv, qkv, f, ft4)


def _gelu_tanh(x):
    c = 0.7978845608028654
    return 0.5 * x * (1.0 + jnp.tanh(c * (x + 0.044715 * (x * x * x))))


def _gm_in_kernel(x_ref, g_ref, sc_ref, sh_ref, w_ref, z_ref, h_ref):
    j = pl.program_id(1)

    @pl.when(j == 0)
    def _():
        for rows in _row_chunks(x_ref.shape[0]):
            _norm_modulate(x_ref, g_ref, sc_ref, sh_ref, h_ref, rows)
            acc = jnp.dot(h_ref[rows, :], w_ref[...], preferred_element_type=F32)
            z_ref[rows, :] = _gelu_tanh(acc).astype(BF16)

    @pl.when(j != 0)
    def _():
        acc = jnp.dot(h_ref[...], w_ref[...], preferred_element_type=F32)
        z_ref[...] = _gelu_tanh(acc).astype(BF16)


def _gm_in_proj(x2, g, sc, sh, w, rows_per_batch):
    m, d = x2.shape
    n = w.shape[1]
    tm = _pick(rows_per_batch, (1024, 512, 256, 128))
    tn = _pick(n, (1024, 512, 256, 128))
    return pl.pallas_call(
        _gm_in_kernel,
        grid=(m // tm, n // tn),
        in_specs=_row_specs(tm, d, rows_per_batch) + [pl.BlockSpec((d, tn), lambda i, j: (0, j))],
        out_specs=pl.BlockSpec((tm, tn), lambda i, j: (i, j)),
        out_shape=jax.ShapeDtypeStruct((m, n), BF16),
        scratch_shapes=[pltpu.VMEM((tm, d), BF16)],
        compiler_params=_params("arbitrary", "arbitrary"),
        name="gmlp_in_proj",
    )(x2, g, sc, sh, w)


def _gm_gate_kernel(u_ref, v_ref, vg_ref, ws_ref, bst_ref, o_ref):
    rows, dg = v_ref.shape
    groups = dg // GM_GROUP
    v = v_ref[...].astype(F32)
    r = lax.rsqrt(jnp.mean(v * v, axis=-1, keepdims=True) + EPS)
    vn = ((v * r) * vg_ref[...]).astype(BF16)
    row = lax.broadcasted_iota(jnp.int32, (CHUNK, CHUNK), 0)
    col = lax.broadcasted_iota(jnp.int32, (CHUNK, CHUNK), 1)
    causal = col <= row
    for g in range(groups):
        w = jnp.where(causal, ws_ref[g], 0.0).astype(BF16)
        bias = bst_ref[:, g:g + 1]
        lo = g * GM_GROUP
        for c in range(rows // CHUNK):
            r0 = c * CHUNK
            sv = jnp.dot(w, vn[r0:r0 + CHUNK, lo:lo + GM_GROUP], preferred_element_type=F32) + bias
            u = u_ref[r0:r0 + CHUNK, lo:lo + GM_GROUP].astype(F32)
            o_ref[r0:r0 + CHUNK, lo:lo + GM_GROUP] = (u * sv).astype(BF16)


def _gm_gate(z, v_g, w_s, b_s_t, rows_per_batch):
    m, n2 = z.shape
    dg = n2 // 2
    groups = dg // GM_GROUP
    tr = _pick(rows_per_batch, (512, 256, 128))
    return pl.pallas_call(
        _gm_gate_kernel,
        grid=(m // tr,),
        in_specs=[
            pl.BlockSpec((tr, dg), lambda i: (i, 0)),
            pl.BlockSpec((tr, dg), lambda i: (i, 1)),
            pl.BlockSpec((1, dg), lambda i: (0, 0)),
            pl.BlockSpec((groups, CHUNK, CHUNK), lambda i: (0, 0, 0)),
            pl.BlockSpec((CHUNK, groups), lambda i: (0, 0)),
        ],
        out_specs=pl.BlockSpec((tr, dg), lambda i: (i, 0)),
        out_shape=jax.ShapeDtypeStruct((m, dg), BF16),
        compiler_params=_params("arbitrary"),
        name="gmlp_spatial_gate",
    )(z, z, v_g, w_s, b_s_t)


def _ffn_in_kernel(x_ref, g_ref, sc_ref, sh_ref, wg_ref, wu_ref, cwg_ref, cwu_ref, cbg_ref, cbu_ref,
                   o_ref, h_ref, carry_g_ref, carry_u_ref, *, tiles_per_batch):
    i = pl.program_id(0)
    j = pl.program_id(1)
    tm = x_ref.shape[0]
    halo = SUBLANES
    rc = min(tm, FFN_ROW_CHUNK)

    @pl.when((i % tiles_per_batch) == 0)
    def _():
        carry_g_ref[j] = jnp.zeros(carry_g_ref.shape[1:], F32)
        carry_u_ref[j] = jnp.zeros(carry_u_ref.shape[1:], F32)

    sub = lax.broadcasted_iota(jnp.int32, (halo, wg_ref.shape[1]), 0)

    def shifted(acc, tail, k):
        moved = pltpu.roll(acc, k, 0)
        head = jnp.where(sub < k, pltpu.roll(tail, k, 0), moved[0:halo, :])
        return jnp.concatenate([head, moved[halo:, :]], axis=0)

    def conv(acc, tail, cw_ref, cb_ref):
        return (cw_ref[0:1, :] * shifted(acc, tail, 2) + cw_ref[1:2, :] * shifted(acc, tail, 1)
                + cw_ref[2:3, :] * acc + cb_ref[...])

    sizes = [rc] * (tm // rc)
    if rc // 2 >= 128 and rc % 16 == 0:
        sizes = sizes[:-1] + [rc // 2, rc // 2]

    def sweep(normalize):
        tail_g = carry_g_ref[j]
        tail_u = carry_u_ref[j]
        r0 = 0
        for n_rows in sizes:
            rows = slice(r0, r0 + n_rows)
            if normalize:
                _norm_modulate(x_ref, g_ref, sc_ref, sh_ref, h_ref, rows)
            h = h_ref[rows, :]
            acc_g = jnp.dot(h, wg_ref[...], preferred_element_type=F32)
            acc_u = jnp.dot(h, wu_ref[...], preferred_element_type=F32)
            gate = conv(acc_g, tail_g, cwg_ref, cbg_ref)
            up = conv(acc_u, tail_u, cwu_ref, cbu_ref)
            o_ref[rows, :] = (_silu(gate) * up).astype(BF16)
            tail_g = acc_g[n_rows - halo:n_rows, :]
            tail_u = acc_u[n_rows - halo:n_rows, :]
            r0 += n_rows
        carry_g_ref[j] = tail_g
        carry_u_ref[j] = tail_u

    @pl.when(j == 0)
    def _():
        sweep(True)

    @pl.when(j != 0)
    def _():
        sweep(False)


def _ffn_in(x2, g, sc, sh, wg, wu, cwg, cwu, cbg, cbu, rows_per_batch):
    m, d = x2.shape
    fp = wg.shape[1]
    tm = _pick(rows_per_batch, (1024, 512, 256, 128))
    tn = _pick(fp, (512, 256, 128))
    nj = fp // tn
    kern = functools.partial(_ffn_in_kernel, tiles_per_batch=rows_per_batch // tm)
    col = lambda i, j: (0, j)
    return pl.pallas_call(
        kern,
        grid=(m // tm, nj),
        in_specs=_row_specs(tm, d, rows_per_batch) + [
            pl.BlockSpec((d, tn), col),
            pl.BlockSpec((d, tn), col),
            pl.BlockSpec((CONV_W, tn), col),
            pl.BlockSpec((CONV_W, tn), col),
            pl.BlockSpec((1, tn), col),
            pl.BlockSpec((1, tn), col),
        ],
        out_specs=pl.BlockSpec((tm, tn), lambda i, j: (i, j)),
        out_shape=jax.ShapeDtypeStruct((m, fp), BF16),
        scratch_shapes=[
            pltpu.VMEM((tm, d), BF16),
            pltpu.VMEM((nj, SUBLANES, tn), F32),
            pltpu.VMEM((nj, SUBLANES, tn), F32),
        ],
        compiler_params=_params("arbitrary", "arbitrary"),
        name="ffn_in_conv_gate",
    )(x2, g, sc, sh, wg, wu, cwg, cwu, cbg, cbu)


def _out_proj_kernel(a_ref, w_ref, x_ref, gate_ref, o_ref):
    y = jnp.dot(a_ref[...], w_ref[...], preferred_element_type=F32)
    o_ref[...] = x_ref[...] + gate_ref[0] * y


def _out_proj_residual(a, w, x2, gate, rows_per_batch):
    m, k = a.shape
    n = w.shape[1]
    tm = _pick(rows_per_batch, (1024, 512, 256, 128))
    tn = _pick(n, (1024, 512, 256, 128))
    if k > 2 * n:
        tn = _pick(n, (512, 256, 128))
    elif 2 * k * n * 2 <= RESIDENT_WEIGHT_BYTES:
        tn = n
        tm = _pick(rows_per_batch, (512, 256, 128))
    tiles_per_batch = rows_per_batch // tm
    return pl.pallas_call(
        _out_proj_kernel,
        grid=(m // tm, n // tn),
        in_specs=[
            pl.BlockSpec((tm, k), lambda i, j: (i, 0)),
            pl.BlockSpec((k, tn), lambda i, j: (0, j)),
            pl.BlockSpec((tm, tn), lambda i, j: (i, j)),
            pl.BlockSpec((1, 1, tn), lambda i, j: (i // tiles_per_batch, 0, j)),
        ],
        out_specs=pl.BlockSpec((tm, tn), lambda i, j: (i, j)),
        out_shape=jax.ShapeDtypeStruct((m, n), F32),
        compiler_params=_params("arbitrary", "arbitrary"),
        name="out_proj_residual",
    )(a, w, x2, gate)


def _final_norm_kernel(x_ref, g_ref, o_ref):
    x = x_ref[...]
    r = lax.rsqrt(jnp.mean(x * x, axis=-1, keepdims=True) + EPS)
    o_ref[...] = (x * r) * g_ref[...]


def _final_norm(x2, g):
    m, d = x2.shape
    tm = _pick(m, (1024, 512, 256, 128))
    return pl.pallas_call(
        _final_norm_kernel,
        grid=(m // tm,),
        in_specs=[pl.BlockSpec((tm, d), lambda i: (i, 0)), pl.BlockSpec((1, d), lambda i: (0, 0))],
        out_specs=pl.BlockSpec((tm, d), lambda i: (i, 0)),
        out_shape=jax.ShapeDtypeStruct((m, d), F32),
        compiler_params=_params("arbitrary"),
        name="final_rms_norm",
    )(x2, g)


def _pad_cols(a, n):
    return jnp.pad(a, ((0, 0), (0, n - a.shape[1])))


def kernel(x, c, mod_w, mod_b, mix_norm_g, ffn_norm_g, attn_w_in, attn_b_f, attn_w_o, gm_w_in, gm_v_g, gm_w_s,
           gm_b_s, gm_w_o, ffn_w_in, ffn_conv_w, ffn_conv_b, ffn_w_out, final_g):
    batch, seq, d = x.shape
    depth = mod_w.shape[0]
    heads = d // HEAD_DIM
    d_ff = ffn_w_in.shape[-1] // 2
    fp = _round_up(d_ff, 512)
    m = batch * seq
    assert d % LANES == 0 and seq % CHUNK == 0 and heads <= LANES
    t_attn = _pick(seq, (512, 256, 128))

    x2 = x.reshape(m, d)
    mod = _modulation(c, mod_w, mod_b).reshape(depth, batch, 6, 1, d)

    for i in range(depth):
        sh1, sc1, g1, sh2, sc2, g2 = (mod[i, :, k] for k in range(6))
        j = i // 2
        gm = mix_norm_g[i].reshape(1, d)
        if i % 2 == 0:
            w_in = attn_w_in[j]
            w_qkv = w_in[:, :3 * d].astype(BF16)
            w_f = _pad_cols(w_in[:, 3 * d:], LANES).astype(BF16)
            b_f = _pad_cols(attn_b_f[j].reshape(1, heads), LANES)
            qkv, flog = _attn_in_proj(x2, gm, sc1, sh1, w_qkv, w_f, b_f, seq)
            f, ft = _forget_cumsum(flog, batch, seq)
            ft4 = ft[:, :heads, :].reshape(batch, heads, seq // t_attn, t_attn)
            y = _fox_attention(qkv, f, ft4, batch, seq, d)
            w_o = attn_w_o[j].astype(BF16)
        else:
            z = _gm_in_proj(x2, gm, sc1, sh1, gm_w_in[j].astype(BF16), seq)
            y = _gm_gate(z, gm_v_g[j].reshape(1, -1), gm_w_s[j], gm_b_s[j].T, seq)
            w_o = gm_w_o[j].astype(BF16)
        x2 = _out_proj_residual(y, w_o, x2, g1, seq)

        w_in = ffn_w_in[i]
        wg = _pad_cols(w_in[:, :d_ff], fp).astype(BF16)
        wu = _pad_cols(w_in[:, d_ff:], fp).astype(BF16)
        cw = ffn_conv_w[i]
        cb = ffn_conv_b[i].reshape(1, 2 * d_ff)
        a = _ffn_in(x2, ffn_norm_g[i].reshape(1, d), sc2, sh2, wg, wu,
                    _pad_cols(cw[:, :d_ff], fp), _pad_cols(cw[:, d_ff:], fp),
                    _pad_cols(cb[:, :d_ff], fp), _pad_cols(cb[:, d_ff:], fp), seq)
        w_out = jnp.pad(ffn_w_out[i], ((0, fp - d_ff), (0, 0))).astype(BF16)
        x2 = _out_proj_residual(a, w_out, x2, g2, seq)

    return _final_norm(x2, final_g.reshape(1, d)).reshape(batch, seq, d)
```

```python
import functools

import jax
import jax.numpy as jnp
from jax import lax
from jax.experimental import pallas as pl
from jax.experimental.pallas import tpu as pltpu

HEAD_DIM = 128
CHUNK = 128
GM_GROUP = 128
CONV_W = 3
EPS = 1e-6
LOG2E = 1.4426950408889634
LANES = 128
SUBLANES = 8
VMEM_LIMIT_BYTES = 56 * 1024 * 1024
RESIDENT_WEIGHT_BYTES = 16 * 1024 * 1024
FAST_PATH_LIMIT = 1e30
FFN_ROW_CHUNK = 128
NORM_ROW_CHUNK = 256

F32 = jnp.float32
BF16 = jnp.bfloat16


def _pick(n, prefs):
    for p in prefs:
        if n % p == 0:
            return p
    return n


def _round_up(n, m):
    return (n + m - 1) // m * m


def _params(*sem):
    return pltpu.CompilerParams(dimension_semantics=sem, vmem_limit_bytes=VMEM_LIMIT_BYTES)


def _silu(x):
    return x / (1.0 + jnp.exp(-x))


def _mod_kernel(c_ref, w_ref, b_ref, o_ref):
    ca = _silu(c_ref[...]).astype(BF16)
    o_ref[0] = jnp.dot(ca, w_ref[0].astype(BF16), preferred_element_type=F32) + b_ref[0]


def _modulation(c, mod_w, mod_b):
    depth, d, n = mod_w.shape
    b = c.shape[0]
    tn = _pick(n, (1024, 512, 256, 128))
    return pl.pallas_call(
        _mod_kernel,
        grid=(depth, n // tn),
        in_specs=[
            pl.BlockSpec((b, d), lambda l, j: (0, 0)),
            pl.BlockSpec((1, d, tn), lambda l, j: (l, 0, j)),
            pl.BlockSpec((1, 1, tn), lambda l, j: (l, 0, j)),
        ],
        out_specs=pl.BlockSpec((1, b, tn), lambda l, j: (l, 0, j)),
        out_shape=jax.ShapeDtypeStruct((depth, b, n), F32),
        compiler_params=_params("arbitrary", "arbitrary"),
        name="adaln_modulation",
    )(c, mod_w, mod_b.reshape(depth, 1, n))


def _norm_modulate(x_ref, g_ref, sc_ref, sh_ref, h_ref, rows=slice(None)):
    x = x_ref[rows, :]
    r = lax.rsqrt(jnp.mean(x * x, axis=-1, keepdims=True) + EPS)
    a = g_ref[...] * (1.0 + sc_ref[0])
    h_ref[rows, :] = ((x * r) * a + sh_ref[0]).astype(BF16)


def _row_chunks(tm):
    rc = min(tm, NORM_ROW_CHUNK)
    return [slice(r0, r0 + rc) for r0 in range(0, tm, rc)]


def _row_specs(tm, d, rows_per_batch):
    tiles_per_batch = rows_per_batch // tm
    return [
        pl.BlockSpec((tm, d), lambda i, j: (i, 0)),
        pl.BlockSpec((1, d), lambda i, j: (0, 0)),
        pl.BlockSpec((1, 1, d), lambda i, j: (i // tiles_per_batch, 0, 0)),
        pl.BlockSpec((1, 1, d), lambda i, j: (i // tiles_per_batch, 0, 0)),
    ]


def _attn_in_kernel(x_ref, g_ref, sc_ref, sh_ref, w_ref, wf_ref, bf_ref, qkv_ref, fl_ref, h_ref,
                    *, q_tiles, q_scale):
    j = pl.program_id(1)
    scale = jnp.where(j < q_tiles, q_scale, 1.0)

    @pl.when(j == 0)
    def _():
        for rows in _row_chunks(x_ref.shape[0]):
            _norm_modulate(x_ref, g_ref, sc_ref, sh_ref, h_ref, rows)
            h = h_ref[rows, :]
            fl_ref[rows, :] = jnp.dot(h, wf_ref[...], preferred_element_type=F32) + bf_ref[...]
            qkv_ref[rows, :] = (jnp.dot(h, w_ref[...], preferred_element_type=F32) * scale).astype(BF16)

    @pl.when(j != 0)
    def _():
        acc = jnp.dot(h_ref[...], w_ref[...], preferred_element_type=F32)
        qkv_ref[...] = (acc * scale).astype(BF16)


def _attn_in_proj(x2, g, sc, sh, w_qkv, w_f, b_f, rows_per_batch):
    m, d = x2.shape
    n = w_qkv.shape[1]
    tm = _pick(rows_per_batch, (1024, 512, 256, 128))
    tn = _pick(d, (2048, 1024, 512, 256, 128))
    kern = functools.partial(_attn_in_kernel, q_tiles=d // tn, q_scale=HEAD_DIM ** -0.5 * LOG2E)
    return pl.pallas_call(
        kern,
        grid=(m // tm, n // tn),
        in_specs=_row_specs(tm, d, rows_per_batch) + [
            pl.BlockSpec((d, tn), lambda i, j: (0, j)),
            pl.BlockSpec((d, LANES), lambda i, j: (0, 0)),
            pl.BlockSpec((1, LANES), lambda i, j: (0, 0)),
        ],
        out_specs=[
            pl.BlockSpec((tm, tn), lambda i, j: (i, j)),
            pl.BlockSpec((tm, LANES), lambda i, j: (i, 0)),
        ],
        out_shape=[
            jax.ShapeDtypeStruct((m, n), BF16),
            jax.ShapeDtypeStruct((m, LANES), F32),
        ],
        scratch_shapes=[pltpu.VMEM((tm, d), BF16)],
        compiler_params=_params("arbitrary", "arbitrary"),
        name="attn_in_proj",
    )(x2, g, sc, sh, w_qkv, w_f, b_f)


def _split3(x):
    hi = x.astype(BF16)
    r1 = x - hi.astype(F32)
    mid = r1.astype(BF16)
    lo = (r1 - mid.astype(F32)).astype(BF16)
    return hi, mid, lo


def _forget_cumsum_kernel(fl_ref, f_ref, ft_ref, carry_ref, *, sub):
    @pl.when(pl.program_id(1) == 0)
    def _():
        carry_ref[...] = jnp.zeros_like(carry_ref)

    tc = fl_ref.shape[0]
    row = lax.broadcasted_iota(jnp.int32, (sub, sub), 0)
    col = lax.broadcasted_iota(jnp.int32, (sub, sub), 1)
    tril = jnp.where(row >= col, 1.0, 0.0).astype(BF16)
    carry = carry_ref[0:1, :]
    for r in range(tc // sub):
        z = fl_ref[r * sub:(r + 1) * sub, :]
        lf = jnp.minimum(z, 0.0) - jnp.log1p(jnp.exp(-jnp.abs(z)))
        hi, mid, lo = _split3(lf)
        c = (jnp.dot(tril, lo, preferred_element_type=F32)
             + jnp.dot(tril, mid, preferred_element_type=F32)
             + jnp.dot(tril, hi, preferred_element_type=F32)) + carry
        c2 = c * LOG2E
        f_ref[r * sub:(r + 1) * sub, :] = c2
        ft_ref[0, :, r * sub:(r + 1) * sub] = c2.T
        carry = c[sub - 1:sub, :]
    carry_ref[0:1, :] = carry


def _forget_cumsum(flog, batch, seq):
    tc = _pick(seq, (1024, 512, 256, 128))
    sub = _pick(tc, (256, 128))
    nt = seq // tc
    return pl.pallas_call(
        functools.partial(_forget_cumsum_kernel, sub=sub),
        grid=(batch, nt),
        in_specs=[pl.BlockSpec((tc, LANES), lambda b, t: (b * nt + t, 0))],
        out_specs=[
            pl.BlockSpec((tc, LANES), lambda b, t: (b * nt + t, 0)),
            pl.BlockSpec((1, LANES, tc), lambda b, t: (b, 0, t)),
        ],
        out_shape=[
            jax.ShapeDtypeStruct((batch * seq, LANES), F32),
            jax.ShapeDtypeStruct((batch, LANES, seq), F32),
        ],
        scratch_shapes=[pltpu.VMEM((SUBLANES, LANES), F32)],
        compiler_params=_params("arbitrary", "arbitrary"),
        name="forget_cumsum",
    )(flog)


def _fox_attn_kernel(q_ref, k_ref, v_ref, f_ref, ft_ref, o_ref, vt_ref, fkb_ref, vmax_ref, m_ref, l_ref, acc_ref,
                     *, t, hg):
    qi = pl.program_id(2)
    nq = vt_ref.shape[1]
    heads = [pl.program_id(1) * hg + g for g in range(hg)]

    @pl.when(qi == 0)
    def _():
        lane = lax.broadcasted_iota(jnp.int32, (t, LANES), 1)
        for g in range(hg):
            v_abs = jnp.abs(v_ref[:, g * HEAD_DIM:(g + 1) * HEAD_DIM].astype(F32))
            vmax_ref[g] = jnp.max(v_abs, axis=0, keepdims=True)
            for c in range(nq):
                rows = slice(c * t, (c + 1) * t)
                vt_ref[g, c] = v_ref[rows, g * HEAD_DIM:(g + 1) * HEAD_DIM].T
                fk = jnp.sum(jnp.where(lane == heads[g], f_ref[rows, :], 0.0), axis=-1, keepdims=True)
                fkb_ref[g, rows, :] = jnp.broadcast_to(fk, (t, LANES))

    qs = [q_ref[:, g * HEAD_DIM:(g + 1) * HEAD_DIM] for g in range(hg)]
    fqs = [ft_ref[0, heads[g], pl.ds(qi, 1), :] for g in range(hg)]

    def scores(g, j):
        start = pl.multiple_of(j * t, t)
        kj = k_ref[pl.ds(start, t), g * HEAD_DIM:(g + 1) * HEAD_DIM]
        s = lax.dot_general(kj, qs[g], (((1,), (1,)), ((), ())), preferred_element_type=F32)
        return s - jnp.tile(fkb_ref[g, pl.ds(start, t), :], (1, t // LANES))

    def update(g, j, s, m_prev, l_prev, acc_prev):
        m_new = jnp.maximum(m_prev, jnp.max(s, axis=0, keepdims=True) + fqs[g])
        alpha = jnp.exp2(m_prev - m_new)
        p = jnp.exp2(s - (m_new - fqs[g]))
        l_new = alpha * l_prev + jnp.sum(p, axis=0, keepdims=True)
        pv = jnp.dot(vt_ref[g, j], p.astype(BF16), preferred_element_type=F32)
        return m_new, l_new, alpha * acc_prev + pv

    def load_state(g):
        return m_ref[g], l_ref[g], acc_ref[g]

    def store_state(g, state):
        m_ref[g], l_ref[g], acc_ref[g] = state

    def exact_body(it, _):
        j = qi - 1 - it
        for g in range(hg):
            store_state(g, update(g, j, scores(g, j), *load_state(g)))
        return 0

    def fast_body(it, _):
        j = qi - 1 - it
        s_next = scores(0, j)
        for g in range(hg):
            s_cur = s_next
            if g + 1 < hg:
                s_next = scores(g + 1, j)
            p = jnp.exp2(s_cur - (m_ref[g] - fqs[g]))
            l_ref[g] += jnp.sum(p, axis=0, keepdims=True)
            acc_ref[g] += jnp.dot(vt_ref[g, j], p.astype(BF16), preferred_element_type=F32)
        return 0

    half = t // 2
    causal = (lax.broadcasted_iota(jnp.int32, (half, half), 0) <= lax.broadcasted_iota(jnp.int32, (half, half), 1))

    def diag_scores(g):
        start = pl.multiple_of(qi * t, t)
        cols = slice(g * HEAD_DIM, (g + 1) * HEAD_DIM)
        nt = (((1,), (1,)), ((), ()))
        sa = lax.dot_general(k_ref[pl.ds(start, half), cols], qs[g], nt, preferred_element_type=F32)
        sa = sa - jnp.tile(fkb_ref[g, pl.ds(start, half), :], (1, t // LANES))
        sb = lax.dot_general(k_ref[pl.ds(start + half, half), cols], qs[g][half:, :], nt,
                             preferred_element_type=F32)
        sb = sb - jnp.tile(fkb_ref[g, pl.ds(start + half, half), :], (1, half // LANES))
        sa = jnp.concatenate([jnp.where(causal, sa[:, :half], -jnp.inf), sa[:, half:]], axis=1)
        return sa, jnp.where(causal, sb, -jnp.inf)

    def late(x_all, x_late, op):
        return jnp.concatenate([x_all[:, :half], op(x_all[:, half:], x_late)], axis=1)

    def diag_update(g, sa, sb):
        m_new = late(jnp.max(sa, axis=0, keepdims=True), jnp.max(sb, axis=0, keepdims=True), jnp.maximum) + fqs[g]
        shift = m_new - fqs[g]
        pa = jnp.exp2(sa - shift)
        pb = jnp.exp2(sb - shift[:, half:])
        l_new = late(jnp.sum(pa, axis=0, keepdims=True), jnp.sum(pb, axis=0, keepdims=True), jnp.add)
        vt = vt_ref[g, qi]
        acc_a = jnp.dot(vt[:, :half], pa.astype(BF16), preferred_element_type=F32)
        acc_b = jnp.dot(vt[:, half:], pb.astype(BF16), preferred_element_type=F32)
        return m_new, l_new, late(acc_a, acc_b, jnp.add)

    def diag_block():
        s_diag = [diag_scores(g) for g in range(hg)]
        for g in range(hg):
            store_state(g, diag_update(g, *s_diag[g]))

    diag_block()
    lax.fori_loop(0, qi, fast_body, 0)

    risk = jnp.zeros((1, t), F32)
    for g in range(hg):
        bound = l_ref[g] * jnp.maximum(jnp.max(vmax_ref[g], axis=-1, keepdims=True), 1.0)
        risk = jnp.maximum(risk, jnp.where(bound < FAST_PATH_LIMIT, 0.0, 1.0))

    @pl.when(jnp.max(risk) > 0.0)
    def _():
        diag_block()
        lax.fori_loop(0, qi, exact_body, 0)

    for g in range(hg):
        inv_l = 1.0 / l_ref[g]
        o_ref[:, g * HEAD_DIM:(g + 1) * HEAD_DIM] = (acc_ref[g] * inv_l).T.astype(BF16)


def _fox_attention(qkv, f, ft4, batch, seq, d):
    heads = d // HEAD_DIM
    hg = _pick(heads, (4, 2, 1))
    wg = hg * HEAD_DIM
    ng = heads // hg
    t = ft4.shape[-1]
    assert t % (2 * LANES) == 0, "the diagonal block is processed in lane-aligned halves"
    nq = seq // t
    return pl.pallas_call(
        functools.partial(_fox_attn_kernel, t=t, hg=hg),
        grid=(batch, ng, nq),
        in_specs=[
            pl.BlockSpec((t, wg), lambda b, h, i: (b * nq + i, h)),
            pl.BlockSpec((seq, wg), lambda b, h, i: (b, ng + h)),
            pl.BlockSpec((seq, wg), lambda b, h, i: (b, 2 * ng + h)),
            pl.BlockSpec((seq, LANES), lambda b, h, i: (b, 0)),
            pl.BlockSpec((1, heads, nq, t), lambda b, h, i: (b, 0, 0, 0)),
        ],
        out_specs=pl.BlockSpec((t, wg), lambda b, h, i: (b * nq + i, h)),
        out_shape=jax.ShapeDtypeStruct((batch * seq, d), BF16),
        scratch_shapes=[
            pltpu.VMEM((hg, nq, HEAD_DIM, t), BF16),
            pltpu.VMEM((hg, seq, LANES), F32),
            pltpu.VMEM((hg, 1, HEAD_DIM), F32),
            pltpu.VMEM((hg, 1, t), F32),
            pltpu.VMEM((hg, 1, t), F32),
            pltpu.VMEM((hg, HEAD_DIM, t), F32),
        ],
        compiler_params=_params("arbitrary", "arbitrary", "arbitrary"),
        name="fox_attention",
    )(qkv, qkv, qkv, f, ft4)


def _gelu_tanh(x):
    c = 0.7978845608028654
    return 0.5 * x * (1.0 + jnp.tanh(c * (x + 0.044715 * (x * x * x))))


def _gm_in_kernel(x_ref, g_ref, sc_ref, sh_ref, w_ref, z_ref, h_ref):
    j = pl.program_id(1)

    @pl.when(j == 0)
    def _():
        for rows in _row_chunks(x_ref.shape[0]):
            _norm_modulate(x_ref, g_ref, sc_ref, sh_ref, h_ref, rows)
            acc = jnp.dot(h_ref[rows, :], w_ref[...], preferred_element_type=F32)
            z_ref[rows, :] = _gelu_tanh(acc).astype(BF16)

    @pl.when(j != 0)
    def _():
        acc = jnp.dot(h_ref[...], w_ref[...], preferred_element_type=F32)
        z_ref[...] = _gelu_tanh(acc).astype(BF16)


def _gm_in_proj(x2, g, sc, sh, w, rows_per_batch):
    m, d = x2.shape
    n = w.shape[1]
    tm = _pick(rows_per_batch, (1024, 512, 256, 128))
    tn = _pick(n, (2048, 1024, 512, 256, 128))
    return pl.pallas_call(
        _gm_in_kernel,
        grid=(m // tm, n // tn),
        in_specs=_row_specs(tm, d, rows_per_batch) + [pl.BlockSpec((d, tn), lambda i, j: (0, j))],
        out_specs=pl.BlockSpec((tm, tn), lambda i, j: (i, j)),
        out_shape=jax.ShapeDtypeStruct((m, n), BF16),
        scratch_shapes=[pltpu.VMEM((tm, d), BF16)],
        compiler_params=_params("arbitrary", "arbitrary"),
        name="gmlp_in_proj",
    )(x2, g, sc, sh, w)


def _gm_gate_kernel(u_ref, v_ref, vg_ref, ws_ref, bst_ref, o_ref):
    rows, dg = v_ref.shape
    groups = dg // GM_GROUP
    v = v_ref[...].astype(F32)
    r = lax.rsqrt(jnp.mean(v * v, axis=-1, keepdims=True) + EPS)
    vn = ((v * r) * vg_ref[...]).astype(BF16)
    row = lax.broadcasted_iota(jnp.int32, (CHUNK, CHUNK), 0)
    col = lax.broadcasted_iota(jnp.int32, (CHUNK, CHUNK), 1)
    causal = col <= row
    for g in range(groups):
        w = jnp.where(causal, ws_ref[g], 0.0).astype(BF16)
        bias = bst_ref[:, g:g + 1]
        lo = g * GM_GROUP
        for c in range(rows // CHUNK):
            r0 = c * CHUNK
            sv = jnp.dot(w, vn[r0:r0 + CHUNK, lo:lo + GM_GROUP], preferred_element_type=F32) + bias
            u = u_ref[r0:r0 + CHUNK, lo:lo + GM_GROUP].astype(F32)
            o_ref[r0:r0 + CHUNK, lo:lo + GM_GROUP] = (u * sv).astype(BF16)


def _gm_gate(z, v_g, w_s, b_s_t, rows_per_batch):
    m, n2 = z.shape
    dg = n2 // 2
    groups = dg // GM_GROUP
    tr = _pick(rows_per_batch, (512, 256, 128))
    return pl.pallas_call(
        _gm_gate_kernel,
        grid=(m // tr,),
        in_specs=[
            pl.BlockSpec((tr, dg), lambda i: (i, 0)),
            pl.BlockSpec((tr, dg), lambda i: (i, 1)),
            pl.BlockSpec((1, dg), lambda i: (0, 0)),
            pl.BlockSpec((groups, CHUNK, CHUNK), lambda i: (0, 0, 0)),
            pl.BlockSpec((CHUNK, groups), lambda i: (0, 0)),
        ],
        out_specs=pl.BlockSpec((tr, dg), lambda i: (i, 0)),
        out_shape=jax.ShapeDtypeStruct((m, dg), BF16),
        compiler_params=_params("arbitrary"),
        name="gmlp_spatial_gate",
    )(z, z, v_g, w_s, b_s_t)


def _ffn_in_kernel(x_ref, g_ref, sc_ref, sh_ref, wg_ref, wu_ref, cwg_ref, cwu_ref, cbg_ref, cbu_ref,
                   o_ref, h_ref, carry_g_ref, carry_u_ref, *, tiles_per_batch):
    i = pl.program_id(0)
    j = pl.program_id(1)
    tm = x_ref.shape[0]
    halo = SUBLANES
    rc = min(tm, FFN_ROW_CHUNK)

    @pl.when((i % tiles_per_batch) == 0)
    def _():
        carry_g_ref[j] = jnp.zeros(carry_g_ref.shape[1:], F32)
        carry_u_ref[j] = jnp.zeros(carry_u_ref.shape[1:], F32)

    sub = lax.broadcasted_iota(jnp.int32, (halo, wg_ref.shape[1]), 0)

    def shifted(acc, tail, k):
        moved = pltpu.roll(acc, k, 0)
        head = jnp.where(sub < k, pltpu.roll(tail, k, 0), moved[0:halo, :])
        return jnp.concatenate([head, moved[halo:, :]], axis=0)

    def conv(acc, tail, cw_ref, cb_ref):
        return (cw_ref[0:1, :] * shifted(acc, tail, 2) + cw_ref[1:2, :] * shifted(acc, tail, 1)
                + cw_ref[2:3, :] * acc + cb_ref[...])

    sizes = [rc] * (tm // rc)
    if rc // 2 >= 128 and rc % 16 == 0:
        sizes = sizes[:-1] + [rc // 2, rc // 2]

    def sweep(normalize):
        tail_g = carry_g_ref[j]
        tail_u = carry_u_ref[j]
        r0 = 0
        for n_rows in sizes:
            rows = slice(r0, r0 + n_rows)
            if normalize:
                _norm_modulate(x_ref, g_ref, sc_ref, sh_ref, h_ref, rows)
            h = h_ref[rows, :]
            acc_g = jnp.dot(h, wg_ref[...], preferred_element_type=F32)
            acc_u = jnp.dot(h, wu_ref[...], preferred_element_type=F32)
            gate = conv(acc_g, tail_g, cwg_ref, cbg_ref)
            up = conv(acc_u, tail_u, cwu_ref, cbu_ref)
            o_ref[rows, :] = (_silu(gate) * up).astype(BF16)
            tail_g = acc_g[n_rows - halo:n_rows, :]
            tail_u = acc_u[n_rows - halo:n_rows, :]
            r0 += n_rows
        carry_g_ref[j] = tail_g
        carry_u_ref[j] = tail_u

    @pl.when(j == 0)
    def _():
        sweep(True)

    @pl.when(j != 0)
    def _():
        sweep(False)


def _ffn_in(x2, g, sc, sh, wg, wu, cwg, cwu, cbg, cbu, rows_per_batch):
    m, d = x2.shape
    fp = wg.shape[1]
    tm = _pick(rows_per_batch, (1024, 512, 256, 128))
    tn = _pick(fp, (512, 256, 128))
    nj = fp // tn
    kern = functools.partial(_ffn_in_kernel, tiles_per_batch=rows_per_batch // tm)
    col = lambda i, j: (0, j)
    return pl.pallas_call(
        kern,
        grid=(m // tm, nj),
        in_specs=_row_specs(tm, d, rows_per_batch) + [
            pl.BlockSpec((d, tn), col),
            pl.BlockSpec((d, tn), col),
            pl.BlockSpec((CONV_W, tn), col),
            pl.BlockSpec((CONV_W, tn), col),
            pl.BlockSpec((1, tn), col),
            pl.BlockSpec((1, tn), col),
        ],
        out_specs=pl.BlockSpec((tm, tn), lambda i, j: (i, j)),
        out_shape=jax.ShapeDtypeStruct((m, fp), BF16),
        scratch_shapes=[
            pltpu.VMEM((tm, d), BF16),
            pltpu.VMEM((nj, SUBLANES, tn), F32),
            pltpu.VMEM((nj, SUBLANES, tn), F32),
        ],
        compiler_params=_params("arbitrary", "arbitrary"),
        name="ffn_in_conv_gate",
    )(x2, g, sc, sh, wg, wu, cwg, cwu, cbg, cbu)


def _out_proj_kernel(a_ref, w_ref, x_ref, gate_ref, o_ref):
    y = jnp.dot(a_ref[...], w_ref[...], preferred_element_type=F32)
    o_ref[...] = x_ref[...] + gate_ref[0] * y


def _out_proj_residual(a, w, x2, gate, rows_per_batch):
    m, k = a.shape
    n = w.shape[1]
    tm = _pick(rows_per_batch, (1024, 512, 256, 128))
    tn = _pick(n, (1024, 512, 256, 128))
    if k > 2 * n:
        tn = _pick(n, (512, 256, 128))
    elif 2 * k * n * 2 <= RESIDENT_WEIGHT_BYTES:
        tn = n
        tm = _pick(rows_per_batch, (512, 256, 128))
    tiles_per_batch = rows_per_batch // tm
    return pl.pallas_call(
        _out_proj_kernel,
        grid=(m // tm, n // tn),
        in_specs=[
            pl.BlockSpec((tm, k), lambda i, j: (i, 0)),
            pl.BlockSpec((k, tn), lambda i, j: (0, j)),
            pl.BlockSpec((tm, tn), lambda i, j: (i, j)),
            pl.BlockSpec((1, 1, tn), lambda i, j: (i // tiles_per_batch, 0, j)),
        ],
        out_specs=pl.BlockSpec((tm, tn), lambda i, j: (i, j)),
        out_shape=jax.ShapeDtypeStruct((m, n), F32),
        compiler_params=_params("arbitrary", "arbitrary"),
        name="out_proj_residual",
    )(a, w, x2, gate)


def _final_norm_kernel(x_ref, g_ref, o_ref):
    x = x_ref[...]
    r = lax.rsqrt(jnp.mean(x * x, axis=-1, keepdims=True) + EPS)
    o_ref[...] = (x * r) * g_ref[...]


def _final_norm(x2, g):
    m, d = x2.shape
    tm = _pick(m, (1024, 512, 256, 128))
    return pl.pallas_call(
        _final_norm_kernel,
        grid=(m // tm,),
        in_specs=[pl.BlockSpec((tm, d), lambda i: (i, 0)), pl.BlockSpec((1, d), lambda i: (0, 0))],
        out_specs=pl.BlockSpec((tm, d), lambda i: (i, 0)),
        out_shape=jax.ShapeDtypeStruct((m, d), F32),
        compiler_params=_params("arbitrary"),
        name="final_rms_norm",
    )(x2, g)


def _pad_cols(a, n):
    return jnp.pad(a, ((0, 0), (0, n - a.shape[1])))


def kernel(x, c, mod_w, mod_b, mix_norm_g, ffn_norm_g, attn_w_in, attn_b_f, attn_w_o, gm_w_in, gm_v_g, gm_w_s,
           gm_b_s, gm_w_o, ffn_w_in, ffn_conv_w, ffn_conv_b, ffn_w_out, final_g):
    batch, seq, d = x.shape
    depth = mod_w.shape[0]
    heads = d // HEAD_DIM
    d_ff = ffn_w_in.shape[-1] // 2
    fp = _round_up(d_ff, 512)
    m = batch * seq
    assert d % LANES == 0 and seq % CHUNK == 0 and heads <= LANES
    t_attn = _pick(seq, (512, 256, 128))

    x2 = x.reshape(m, d)
    mod = _modulation(c, mod_w, mod_b).reshape(depth, batch, 6, 1, d)

    for i in range(depth):
        sh1, sc1, g1, sh2, sc2, g2 = (mod[i, :, k] for k in range(6))
        j = i // 2
        gm = mix_norm_g[i].reshape(1, d)
        if i % 2 == 0:
            w_in = attn_w_in[j]
            w_qkv = w_in[:, :3 * d].astype(BF16)
            w_f = _pad_cols(w_in[:, 3 * d:], LANES).astype(BF16)
            b_f = _pad_cols(attn_b_f[j].reshape(1, heads), LANES)
            qkv, flog = _attn_in_proj(x2, gm, sc1, sh1, w_qkv, w_f, b_f, seq)
            f, ft = _forget_cumsum(flog, batch, seq)
            ft4 = ft[:, :heads, :].reshape(batch, heads, seq // t_attn, t_attn)
            y = _fox_attention(qkv, f, ft4, batch, seq, d)
            w_o = attn_w_o[j].astype(BF16)
        else:
            z = _gm_in_proj(x2, gm, sc1, sh1, gm_w_in[j].astype(BF16), seq)
            y = _gm_gate(z, gm_v_g[j].reshape(1, -1), gm_w_s[j], gm_b_s[j].T, seq)
            w_o = gm_w_o[j].astype(BF16)
        x2 = _out_proj_residual(y, w_o, x2, g1, seq)

        w_in = ffn_w_in[i]
        wg = _pad_cols(w_in[:, :d_ff], fp).astype(BF16)
        wu = _pad_cols(w_in[:, d_ff:], fp).astype(BF16)
        cw = ffn_conv_w[i]
        cb = ffn_conv_b[i].reshape(1, 2 * d_ff)
        a = _ffn_in(x2, ffn_norm_g[i].reshape(1, d), sc2, sh2, wg, wu,
                    _pad_cols(cw[:, :d_ff], fp), _pad_cols(cw[:, d_ff:], fp),
                    _pad_cols(cb[:, :d_ff], fp), _pad_cols(cb[:, d_ff:], fp), seq)
        w_out = jnp.pad(ffn_w_out[i], ((0, fp - d_ff), (0, 0))).astype(BF16)
        x2 = _out_proj_residual(a, w_out, x2, g2, seq)

    return _final_norm(x2, final_g.reshape(1, d)).reshape(batch, seq, d)
```

```python
import functools

import jax
import jax.numpy as jnp
from jax import lax
from jax.experimental import pallas as pl
from jax.experimental.pallas import tpu as pltpu

HEAD_DIM = 128
CHUNK = 128
GM_GROUP = 128
CONV_W = 3
EPS = 1e-6
LOG2E = 1.4426950408889634
LANES = 128
SUBLANES = 8
VMEM_LIMIT_BYTES = 56 * 1024 * 1024
RESIDENT_WEIGHT_BYTES = 16 * 1024 * 1024
FAST_PATH_LIMIT = 1e30
FFN_ROW_CHUNK = 128
FFN_COL_TILE = 512
NORM_ROW_CHUNK = 256

F32 = jnp.float32
BF16 = jnp.bfloat16


def _pick(n, prefs):
    for p in prefs:
        if n % p == 0:
            return p
    return n


def _params(*sem):
    return pltpu.CompilerParams(dimension_semantics=sem, vmem_limit_bytes=VMEM_LIMIT_BYTES)


def _silu(x):
    return x / (1.0 + jnp.exp(-x))


def _mod_kernel(c_ref, w_ref, b_ref, o_ref):
    ca = _silu(c_ref[...]).astype(BF16)
    o_ref[0] = jnp.dot(ca, w_ref[0].astype(BF16), preferred_element_type=F32) + b_ref[0]


def _modulation(c, mod_w, mod_b):
    depth, d, n = mod_w.shape
    b = c.shape[0]
    tn = _pick(n, (1024, 512, 256, 128))
    return pl.pallas_call(
        _mod_kernel,
        grid=(depth, n // tn),
        in_specs=[
            pl.BlockSpec((b, d), lambda l, j: (0, 0)),
            pl.BlockSpec((1, d, tn), lambda l, j: (l, 0, j)),
            pl.BlockSpec((1, 1, tn), lambda l, j: (l, 0, j)),
        ],
        out_specs=pl.BlockSpec((1, b, tn), lambda l, j: (l, 0, j)),
        out_shape=jax.ShapeDtypeStruct((depth, b, n), F32),
        compiler_params=_params("arbitrary", "arbitrary"),
        name="adaln_modulation",
    )(c, mod_w, mod_b.reshape(depth, 1, n))


def _norm_modulate(x_ref, g_ref, sc_ref, sh_ref, h_ref, rows=slice(None)):
    x = x_ref[rows, :]
    r = lax.rsqrt(jnp.mean(x * x, axis=-1, keepdims=True) + EPS)
    a = g_ref[...] * (1.0 + sc_ref[0])
    h_ref[rows, :] = ((x * r) * a + sh_ref[0]).astype(BF16)


def _row_chunks(tm):
    rc = min(tm, NORM_ROW_CHUNK)
    return [slice(r0, r0 + rc) for r0 in range(0, tm, rc)]


def _row_specs(tm, d, rows_per_batch):
    tiles_per_batch = rows_per_batch // tm
    return [
        pl.BlockSpec((tm, d), lambda i, j: (i, 0)),
        pl.BlockSpec((1, d), lambda i, j: (0, 0)),
        pl.BlockSpec((1, 1, d), lambda i, j: (i // tiles_per_batch, 0, 0)),
        pl.BlockSpec((1, 1, d), lambda i, j: (i // tiles_per_batch, 0, 0)),
    ]


def _attn_in_kernel(x_ref, g_ref, sc_ref, sh_ref, w_ref, wf_ref, bf_ref, qkv_ref, fl_ref, h_ref,
                    *, q_tiles, q_scale):
    j = pl.program_id(1)
    scale = jnp.where(j < q_tiles, q_scale, 1.0)

    @pl.when(j == 0)
    def _():
        for rows in _row_chunks(x_ref.shape[0]):
            _norm_modulate(x_ref, g_ref, sc_ref, sh_ref, h_ref, rows)
            h = h_ref[rows, :]
            fl_ref[rows, :] = jnp.dot(h, wf_ref[...], preferred_element_type=F32) + bf_ref[...]
            qkv_ref[rows, :] = (jnp.dot(h, w_ref[...], preferred_element_type=F32) * scale).astype(BF16)

    @pl.when(j != 0)
    def _():
        acc = jnp.dot(h_ref[...], w_ref[...], preferred_element_type=F32)
        qkv_ref[...] = (acc * scale).astype(BF16)


def _attn_in_proj(x2, g, sc, sh, w_in_all, layer, w_f, b_f, rows_per_batch):
    m, d = x2.shape
    n = 3 * d
    tm = _pick(rows_per_batch, (1024, 512, 256, 128))
    tn = _pick(d, (2048, 1024, 512, 256, 128))
    kern = functools.partial(_attn_in_kernel, q_tiles=d // tn, q_scale=HEAD_DIM ** -0.5 * LOG2E)
    return pl.pallas_call(
        kern,
        grid=(m // tm, n // tn),
        in_specs=_row_specs(tm, d, rows_per_batch) + [
            pl.BlockSpec((None, d, tn), lambda i, j: (layer, 0, j)),
            pl.BlockSpec((d, LANES), lambda i, j: (0, 0)),
            pl.BlockSpec((1, LANES), lambda i, j: (0, 0)),
        ],
        out_specs=[
            pl.BlockSpec((tm, tn), lambda i, j: (i, j)),
            pl.BlockSpec((tm, LANES), lambda i, j: (i, 0)),
        ],
        out_shape=[
            jax.ShapeDtypeStruct((m, n), BF16),
            jax.ShapeDtypeStruct((m, LANES), F32),
        ],
        scratch_shapes=[pltpu.VMEM((tm, d), BF16)],
        compiler_params=_params("arbitrary", "arbitrary"),
        name="attn_in_proj",
    )(x2, g, sc, sh, w_in_all, w_f, b_f)


def _split3(x):
    hi = x.astype(BF16)
    r1 = x - hi.astype(F32)
    mid = r1.astype(BF16)
    lo = (r1 - mid.astype(F32)).astype(BF16)
    return hi, mid, lo


def _forget_cumsum_kernel(fl_ref, f_ref, ft_ref, carry_ref, *, sub):
    @pl.when(pl.program_id(1) == 0)
    def _():
        carry_ref[...] = jnp.zeros_like(carry_ref)

    tc = fl_ref.shape[0]
    row = lax.broadcasted_iota(jnp.int32, (sub, sub), 0)
    col = lax.broadcasted_iota(jnp.int32, (sub, sub), 1)
    tril = jnp.where(row >= col, 1.0, 0.0).astype(BF16)
    carry = carry_ref[0:1, :]
    for r in range(tc // sub):
        z = fl_ref[r * sub:(r + 1) * sub, :]
        lf = jnp.minimum(z, 0.0) - jnp.log1p(jnp.exp(-jnp.abs(z)))
        hi, mid, lo = _split3(lf)
        c = (jnp.dot(tril, lo, preferred_element_type=F32)
             + jnp.dot(tril, mid, preferred_element_type=F32)
             + jnp.dot(tril, hi, preferred_element_type=F32)) + carry
        c2 = c * LOG2E
        f_ref[r * sub:(r + 1) * sub, :] = c2
        ft_ref[0, :, r * sub:(r + 1) * sub] = c2.T
        carry = c[sub - 1:sub, :]
    carry_ref[0:1, :] = carry


def _forget_cumsum(flog, batch, seq):
    tc = _pick(seq, (1024, 512, 256, 128))
    sub = _pick(tc, (256, 128))
    nt = seq // tc
    return pl.pallas_call(
        functools.partial(_forget_cumsum_kernel, sub=sub),
        grid=(batch, nt),
        in_specs=[pl.BlockSpec((tc, LANES), lambda b, t: (b * nt + t, 0))],
        out_specs=[
            pl.BlockSpec((tc, LANES), lambda b, t: (b * nt + t, 0)),
            pl.BlockSpec((1, LANES, tc), lambda b, t: (b, 0, t)),
        ],
        out_shape=[
            jax.ShapeDtypeStruct((batch * seq, LANES), F32),
            jax.ShapeDtypeStruct((batch, LANES, seq), F32),
        ],
        scratch_shapes=[pltpu.VMEM((SUBLANES, LANES), F32)],
        compiler_params=_params("arbitrary", "arbitrary"),
        name="forget_cumsum",
    )(flog)


def _fox_attn_kernel(q_ref, k_ref, v_ref, f_ref, ft_ref, o_ref, vt_ref, fkb_ref, vmax_ref, m_ref, l_ref, acc_ref,
                     *, t, hg):
    qi = pl.program_id(2)
    nq = vt_ref.shape[1]
    heads = [pl.program_id(1) * hg + g for g in range(hg)]

    @pl.when(qi == 0)
    def _():
        lane = lax.broadcasted_iota(jnp.int32, (t, LANES), 1)
        for g in range(hg):
            v_abs = jnp.abs(v_ref[:, g * HEAD_DIM:(g + 1) * HEAD_DIM].astype(F32))
            vmax_ref[g] = jnp.max(v_abs, axis=0, keepdims=True)
            for c in range(nq):
                rows = slice(c * t, (c + 1) * t)
                vt_ref[g, c] = v_ref[rows, g * HEAD_DIM:(g + 1) * HEAD_DIM].T
                fk = jnp.sum(jnp.where(lane == heads[g], f_ref[rows, :], 0.0), axis=-1, keepdims=True)
                fkb_ref[g, rows, :] = jnp.broadcast_to(fk, (t, LANES))

    qs = [q_ref[:, g * HEAD_DIM:(g + 1) * HEAD_DIM] for g in range(hg)]
    fqs = [ft_ref[0, heads[g], pl.ds(qi, 1), :] for g in range(hg)]

    def scores(g, j):
        start = pl.multiple_of(j * t, t)
        kj = k_ref[pl.ds(start, t), g * HEAD_DIM:(g + 1) * HEAD_DIM]
        s = lax.dot_general(kj, qs[g], (((1,), (1,)), ((), ())), preferred_element_type=F32)
        return s - jnp.tile(fkb_ref[g, pl.ds(start, t), :], (1, t // LANES))

    def update(g, j, s, m_prev, l_prev, acc_prev):
        m_new = jnp.maximum(m_prev, jnp.max(s, axis=0, keepdims=True) + fqs[g])
        alpha = jnp.exp2(m_prev - m_new)
        p = jnp.exp2(s - (m_new - fqs[g]))
        l_new = alpha * l_prev + jnp.sum(p, axis=0, keepdims=True)
        pv = jnp.dot(vt_ref[g, j], p.astype(BF16), preferred_element_type=F32)
        return m_new, l_new, alpha * acc_prev + pv

    def load_state(g):
        return m_ref[g], l_ref[g], acc_ref[g]

    def store_state(g, state):
        m_ref[g], l_ref[g], acc_ref[g] = state

    def exact_body(it, _):
        j = qi - 1 - it
        for g in range(hg):
            store_state(g, update(g, j, scores(g, j), *load_state(g)))
        return 0

    def fast_body(it, _):
        j = qi - 1 - it
        s_next = scores(0, j)
        for g in range(hg):
            s_cur = s_next
            if g + 1 < hg:
                s_next = scores(g + 1, j)
            p = jnp.exp2(s_cur - (m_ref[g] - fqs[g]))
            l_ref[g] += jnp.sum(p, axis=0, keepdims=True)
            acc_ref[g] += jnp.dot(vt_ref[g, j], p.astype(BF16), preferred_element_type=F32)
        return 0

    half = t // 2
    causal = (lax.broadcasted_iota(jnp.int32, (half, half), 0) <= lax.broadcasted_iota(jnp.int32, (half, half), 1))

    def diag_scores(g):
        start = pl.multiple_of(qi * t, t)
        cols = slice(g * HEAD_DIM, (g + 1) * HEAD_DIM)
        nt = (((1,), (1,)), ((), ()))
        sa = lax.dot_general(k_ref[pl.ds(start, half), cols], qs[g], nt, preferred_element_type=F32)
        sa = sa - jnp.tile(fkb_ref[g, pl.ds(start, half), :], (1, t // LANES))
        sb = lax.dot_general(k_ref[pl.ds(start + half, half), cols], qs[g][half:, :], nt,
                             preferred_element_type=F32)
        sb = sb - jnp.tile(fkb_ref[g, pl.ds(start + half, half), :], (1, half // LANES))
        sa = jnp.concatenate([jnp.where(causal, sa[:, :half], -jnp.inf), sa[:, half:]], axis=1)
        return sa, jnp.where(causal, sb, -jnp.inf)

    def late(x_all, x_late, op):
        return jnp.concatenate([x_all[:, :half], op(x_all[:, half:], x_late)], axis=1)

    def diag_update(g, sa, sb):
        m_new = late(jnp.max(sa, axis=0, keepdims=True), jnp.max(sb, axis=0, keepdims=True), jnp.maximum) + fqs[g]
        shift = m_new - fqs[g]
        pa = jnp.exp2(sa - shift)
        pb = jnp.exp2(sb - shift[:, half:])
        l_new = late(jnp.sum(pa, axis=0, keepdims=True), jnp.sum(pb, axis=0, keepdims=True), jnp.add)
        vt = vt_ref[g, qi]
        acc_a = jnp.dot(vt[:, :half], pa.astype(BF16), preferred_element_type=F32)
        acc_b = jnp.dot(vt[:, half:], pb.astype(BF16), preferred_element_type=F32)
        return m_new, l_new, late(acc_a, acc_b, jnp.add)

    def diag_block():
        s_diag = [diag_scores(g) for g in range(hg)]
        for g in range(hg):
            store_state(g, diag_update(g, *s_diag[g]))

    diag_block()
    lax.fori_loop(0, qi, fast_body, 0)

    risk = jnp.zeros((1, t), F32)
    for g in range(hg):
        bound = l_ref[g] * jnp.maximum(jnp.max(vmax_ref[g], axis=-1, keepdims=True), 1.0)
        risk = jnp.maximum(risk, jnp.where(bound < FAST_PATH_LIMIT, 0.0, 1.0))

    @pl.when(jnp.max(risk) > 0.0)
    def _():
        diag_block()
        lax.fori_loop(0, qi, exact_body, 0)

    for g in range(hg):
        inv_l = 1.0 / l_ref[g]
        o_ref[:, g * HEAD_DIM:(g + 1) * HEAD_DIM] = (acc_ref[g] * inv_l).T.astype(BF16)


def _fox_attention(qkv, f, ft4, batch, seq, d):
    heads = d // HEAD_DIM
    hg = _pick(heads, (4, 2, 1))
    wg = hg * HEAD_DIM
    ng = heads // hg
    t = ft4.shape[-1]
    assert t % (2 * LANES) == 0, "the diagonal block is processed in lane-aligned halves"
    nq = seq // t
    return pl.pallas_call(
        functools.partial(_fox_attn_kernel, t=t, hg=hg),
        grid=(batch, ng, nq),
        in_specs=[
            pl.BlockSpec((t, wg), lambda b, h, i: (b * nq + i, h)),
            pl.BlockSpec((seq, wg), lambda b, h, i: (b, ng + h)),
            pl.BlockSpec((seq, wg), lambda b, h, i: (b, 2 * ng + h)),
            pl.BlockSpec((seq, LANES), lambda b, h, i: (b, 0)),
            pl.BlockSpec((1, heads, nq, t), lambda b, h, i: (b, 0, 0, 0)),
        ],
        out_specs=pl.BlockSpec((t, wg), lambda b, h, i: (b * nq + i, h)),
        out_shape=jax.ShapeDtypeStruct((batch * seq, d), BF16),
        scratch_shapes=[
            pltpu.VMEM((hg, nq, HEAD_DIM, t), BF16),
            pltpu.VMEM((hg, seq, LANES), F32),
            pltpu.VMEM((hg, 1, HEAD_DIM), F32),
            pltpu.VMEM((hg, 1, t), F32),
            pltpu.VMEM((hg, 1, t), F32),
            pltpu.VMEM((hg, HEAD_DIM, t), F32),
        ],
        compiler_params=_params("arbitrary", "arbitrary", "arbitrary"),
        name="fox_attention",
    )(qkv, qkv, qkv, f, ft4)


def _gelu_tanh(x):
    c = 0.7978845608028654
    return 0.5 * x * (1.0 + jnp.tanh(c * (x + 0.044715 * (x * x * x))))


def _gm_in_kernel(x_ref, g_ref, sc_ref, sh_ref, w_ref, z_ref, h_ref):
    j = pl.program_id(1)

    @pl.when(j == 0)
    def _():
        for rows in _row_chunks(x_ref.shape[0]):
            _norm_modulate(x_ref, g_ref, sc_ref, sh_ref, h_ref, rows)
            acc = jnp.dot(h_ref[rows, :], w_ref[...], preferred_element_type=F32)
            z_ref[rows, :] = _gelu_tanh(acc).astype(BF16)

    @pl.when(j != 0)
    def _():
        acc = jnp.dot(h_ref[...], w_ref[...], preferred_element_type=F32)
        z_ref[...] = _gelu_tanh(acc).astype(BF16)


def _gm_in_proj(x2, g, sc, sh, w_all, layer, rows_per_batch):
    m, d = x2.shape
    n = w_all.shape[2]
    tm = _pick(rows_per_batch, (1024, 512, 256, 128))
    tn = _pick(n, (2048, 1024, 512, 256, 128))
    return pl.pallas_call(
        _gm_in_kernel,
        grid=(m // tm, n // tn),
        in_specs=_row_specs(tm, d, rows_per_batch) + [
            pl.BlockSpec((None, d, tn), lambda i, j: (layer, 0, j))],
        out_specs=pl.BlockSpec((tm, tn), lambda i, j: (i, j)),
        out_shape=jax.ShapeDtypeStruct((m, n), BF16),
        scratch_shapes=[pltpu.VMEM((tm, d), BF16)],
        compiler_params=_params("arbitrary", "arbitrary"),
        name="gmlp_in_proj",
    )(x2, g, sc, sh, w_all)


def _gm_gate_kernel(u_ref, v_ref, vg_ref, ws_ref, bst_ref, o_ref):
    rows, dg = v_ref.shape
    groups = dg // GM_GROUP
    v = v_ref[...].astype(F32)
    r = lax.rsqrt(jnp.mean(v * v, axis=-1, keepdims=True) + EPS)
    vn = ((v * r) * vg_ref[...]).astype(BF16)
    row = lax.broadcasted_iota(jnp.int32, (CHUNK, CHUNK), 0)
    col = lax.broadcasted_iota(jnp.int32, (CHUNK, CHUNK), 1)
    causal = col <= row
    for g in range(groups):
        w = jnp.where(causal, ws_ref[g], 0.0).astype(BF16)
        bias = bst_ref[:, g:g + 1]
        lo = g * GM_GROUP
        for c in range(rows // CHUNK):
            r0 = c * CHUNK
            sv = jnp.dot(w, vn[r0:r0 + CHUNK, lo:lo + GM_GROUP], preferred_element_type=F32) + bias
            u = u_ref[r0:r0 + CHUNK, lo:lo + GM_GROUP].astype(F32)
            o_ref[r0:r0 + CHUNK, lo:lo + GM_GROUP] = (u * sv).astype(BF16)


def _gm_gate(z, v_g, w_s, b_s_t, rows_per_batch):
    m, n2 = z.shape
    dg = n2 // 2
    groups = dg // GM_GROUP
    tr = _pick(rows_per_batch, (512, 256, 128))
    return pl.pallas_call(
        _gm_gate_kernel,
        grid=(m // tr,),
        in_specs=[
            pl.BlockSpec((tr, dg), lambda i: (i, 0)),
            pl.BlockSpec((tr, dg), lambda i: (i, 1)),
            pl.BlockSpec((1, dg), lambda i: (0, 0)),
            pl.BlockSpec((groups, CHUNK, CHUNK), lambda i: (0, 0, 0)),
            pl.BlockSpec((CHUNK, groups), lambda i: (0, 0)),
        ],
        out_specs=pl.BlockSpec((tr, dg), lambda i: (i, 0)),
        out_shape=jax.ShapeDtypeStruct((m, dg), BF16),
        compiler_params=_params("arbitrary"),
        name="gmlp_spatial_gate",
    )(z, z, v_g, w_s, b_s_t)


def _ffn_in_kernel(x_ref, g_ref, sc_ref, sh_ref, wg_ref, wu_ref, cwg_ref, cwu_ref, cbg_ref, cbu_ref,
                   o_ref, h_ref, carry_g_ref, carry_u_ref, *, tiles_per_batch):
    i = pl.program_id(0)
    j = pl.program_id(1)
    tm = x_ref.shape[0]
    halo = SUBLANES
    rc = min(tm, FFN_ROW_CHUNK)

    @pl.when((i % tiles_per_batch) == 0)
    def _():
        carry_g_ref[j] = jnp.zeros(carry_g_ref.shape[1:], F32)
        carry_u_ref[j] = jnp.zeros(carry_u_ref.shape[1:], F32)

    sub = lax.broadcasted_iota(jnp.int32, (halo, wg_ref.shape[1]), 0)

    def shifted(acc, tail, k):
        moved = pltpu.roll(acc, k, 0)
        head = jnp.where(sub < k, pltpu.roll(tail, k, 0), moved[0:halo, :])
        return jnp.concatenate([head, moved[halo:, :]], axis=0)

    def conv(acc, tail, cw_ref, cb_ref):
        return (cw_ref[0:1, :] * shifted(acc, tail, 2) + cw_ref[1:2, :] * shifted(acc, tail, 1)
                + cw_ref[2:3, :] * acc + cb_ref[...])

    sizes = [rc] * (tm // rc)
    if rc // 2 >= 128 and rc % 16 == 0:
        sizes = sizes[:-1] + [rc // 2, rc // 2]

    def sweep(normalize):
        tail_g = carry_g_ref[j]
        tail_u = carry_u_ref[j]
        r0 = 0
        for n_rows in sizes:
            rows = slice(r0, r0 + n_rows)
            if normalize:
                _norm_modulate(x_ref, g_ref, sc_ref, sh_ref, h_ref, rows)
            h = h_ref[rows, :]
            acc_g = jnp.dot(h, wg_ref[...], preferred_element_type=F32)
            acc_u = jnp.dot(h, wu_ref[...], preferred_element_type=F32)
            gate = conv(acc_g, tail_g, cwg_ref, cbg_ref)
            up = conv(acc_u, tail_u, cwu_ref, cbu_ref)
            o_ref[rows, :] = (_silu(gate) * up).astype(BF16)
            tail_g = acc_g[n_rows - halo:n_rows, :]
            tail_u = acc_u[n_rows - halo:n_rows, :]
            r0 += n_rows
        carry_g_ref[j] = tail_g
        carry_u_ref[j] = tail_u

    @pl.when(j == 0)
    def _():
        sweep(True)

    @pl.when(j != 0)
    def _():
        sweep(False)


def _ffn_in(x2, g, sc, sh, wg_all, wu_all, layer, cwg, cwu, cbg, cbu, rows_per_batch):
    m, d = x2.shape
    fp = wg_all.shape[2]
    tm = _pick(rows_per_batch, (1024, 512, 256, 128))
    tn = FFN_COL_TILE
    nj = fp // tn
    kern = functools.partial(_ffn_in_kernel, tiles_per_batch=rows_per_batch // tm)
    col = lambda i, j: (0, j)
    wcol = lambda i, j: (layer, 0, j)
    return pl.pallas_call(
        kern,
        grid=(m // tm, nj),
        in_specs=_row_specs(tm, d, rows_per_batch) + [
            pl.BlockSpec((None, d, tn), wcol),
            pl.BlockSpec((None, d, tn), wcol),
            pl.BlockSpec((CONV_W, tn), col),
            pl.BlockSpec((CONV_W, tn), col),
            pl.BlockSpec((1, tn), col),
            pl.BlockSpec((1, tn), col),
        ],
        out_specs=pl.BlockSpec((tm, tn), lambda i, j: (i, j)),
        out_shape=jax.ShapeDtypeStruct((m, fp), BF16),
        scratch_shapes=[
            pltpu.VMEM((tm, d), BF16),
            pltpu.VMEM((nj, SUBLANES, tn), F32),
            pltpu.VMEM((nj, SUBLANES, tn), F32),
        ],
        compiler_params=_params("arbitrary", "arbitrary"),
        name="ffn_in_conv_gate",
    )(x2, g, sc, sh, wg_all, wu_all, cwg, cwu, cbg, cbu)


def _out_proj_kernel(a_ref, w_ref, x_ref, gate_ref, o_ref):
    y = jnp.dot(a_ref[...], w_ref[...], preferred_element_type=F32)
    o_ref[...] = x_ref[...] + gate_ref[0] * y


def _out_proj_residual(a, w_all, layer, x2, gate, rows_per_batch):
    m, k = a.shape
    n = w_all.shape[2]
    tm = _pick(rows_per_batch, (1024, 512, 256, 128))
    tn = _pick(n, (1024, 512, 256, 128))
    if k > 2 * n:
        tn = _pick(n, (512, 256, 128))
    elif 2 * k * n * 2 <= RESIDENT_WEIGHT_BYTES:
        tn = n
        tm = _pick(rows_per_batch, (512, 256, 128))
    tiles_per_batch = rows_per_batch // tm
    return pl.pallas_call(
        _out_proj_kernel,
        grid=(m // tm, n // tn),
        in_specs=[
            pl.BlockSpec((tm, k), lambda i, j: (i, 0)),
            pl.BlockSpec((None, k, tn), lambda i, j: (layer, 0, j)),
            pl.BlockSpec((tm, tn), lambda i, j: (i, j)),
            pl.BlockSpec((1, 1, tn), lambda i, j: (i // tiles_per_batch, 0, j)),
        ],
        out_specs=pl.BlockSpec((tm, tn), lambda i, j: (i, j)),
        out_shape=jax.ShapeDtypeStruct((m, n), F32),
        compiler_params=_params("arbitrary", "arbitrary"),
        name="out_proj_residual",
    )(a, w_all, x2, gate)


def _final_norm_kernel(x_ref, g_ref, o_ref):
    x = x_ref[...]
    r = lax.rsqrt(jnp.mean(x * x, axis=-1, keepdims=True) + EPS)
    o_ref[...] = (x * r) * g_ref[...]


def _final_norm(x2, g):
    m, d = x2.shape
    tm = _pick(m, (1024, 512, 256, 128))
    return pl.pallas_call(
        _final_norm_kernel,
        grid=(m // tm,),
        in_specs=[pl.BlockSpec((tm, d), lambda i: (i, 0)), pl.BlockSpec((1, d), lambda i: (0, 0))],
        out_specs=pl.BlockSpec((tm, d), lambda i: (i, 0)),
        out_shape=jax.ShapeDtypeStruct((m, d), F32),
        compiler_params=_params("arbitrary"),
        name="final_rms_norm",
    )(x2, g)


def _pad_cols(a, n):
    return jnp.pad(a, ((0, 0), (0, n - a.shape[1])))


def kernel(x, c, mod_w, mod_b, mix_norm_g, ffn_norm_g, attn_w_in, attn_b_f, attn_w_o, gm_w_in, gm_v_g, gm_w_s,
           gm_b_s, gm_w_o, ffn_w_in, ffn_conv_w, ffn_conv_b, ffn_w_out, final_g):
    batch, seq, d = x.shape
    depth = mod_w.shape[0]
    heads = d // HEAD_DIM
    d_ff = ffn_w_in.shape[-1] // 2
    fp = -(-d_ff // FFN_COL_TILE) * FFN_COL_TILE
    m = batch * seq
    assert d % LANES == 0 and seq % CHUNK == 0 and heads <= LANES
    t_attn = _pick(seq, (512, 256, 128))

    x2 = x.reshape(m, d)
    mod = _modulation(c, mod_w, mod_b).reshape(depth, batch, 6, 1, d)

    pad_ff = ((0, 0), (0, 0), (0, fp - d_ff))
    attn_w_in_b = attn_w_in.astype(BF16)
    attn_w_o_b = attn_w_o.astype(BF16)
    gm_w_in_b = gm_w_in.astype(BF16)
    gm_w_o_b = gm_w_o.astype(BF16)
    ffn_wg_b = jnp.pad(ffn_w_in[:, :, :d_ff], pad_ff).astype(BF16)
    ffn_wu_b = jnp.pad(ffn_w_in[:, :, d_ff:], pad_ff).astype(BF16)
    ffn_w_out_b = jnp.pad(ffn_w_out, ((0, 0), (0, fp - d_ff), (0, 0))).astype(BF16)

    for i in range(depth):
        sh1, sc1, g1, sh2, sc2, g2 = (mod[i, :, k] for k in range(6))
        j = i // 2
        gm = mix_norm_g[i].reshape(1, d)
        if i % 2 == 0:
            w_f = _pad_cols(attn_w_in[j][:, 3 * d:], LANES).astype(BF16)
            b_f = _pad_cols(attn_b_f[j].reshape(1, heads), LANES)
            qkv, flog = _attn_in_proj(x2, gm, sc1, sh1, attn_w_in_b, j, w_f, b_f, seq)
            f, ft = _forget_cumsum(flog, batch, seq)
            ft4 = ft[:, :heads, :].reshape(batch, heads, seq // t_attn, t_attn)
            y = _fox_attention(qkv, f, ft4, batch, seq, d)
            x2 = _out_proj_residual(y, attn_w_o_b, j, x2, g1, seq)
        else:
            z = _gm_in_proj(x2, gm, sc1, sh1, gm_w_in_b, j, seq)
            y = _gm_gate(z, gm_v_g[j].reshape(1, -1), gm_w_s[j], gm_b_s[j].T, seq)
            x2 = _out_proj_residual(y, gm_w_o_b, j, x2, g1, seq)

        cw = ffn_conv_w[i]
        cb = ffn_conv_b[i].reshape(1, 2 * d_ff)
        a = _ffn_in(x2, ffn_norm_g[i].reshape(1, d), sc2, sh2, ffn_wg_b, ffn_wu_b, i,
                    _pad_cols(cw[:, :d_ff], fp), _pad_cols(cw[:, d_ff:], fp),
                    _pad_cols(cb[:, :d_ff], fp), _pad_cols(cb[:, d_ff:], fp), seq)
        x2 = _out_proj_residual(a, ffn_w_out_b, i, x2, g2, seq)

    return _final_norm(x2, final_g.reshape(1, d)).reshape(batch, seq, d)
```

```python
import functools

import jax
import jax.numpy as jnp
from jax import lax
from jax.experimental import pallas as pl
from jax.experimental.pallas import tpu as pltpu

HEAD_DIM = 128
CHUNK = 128
GM_GROUP = 128
CONV_W = 3
EPS = 1e-6
LOG2E = 1.4426950408889634
LANES = 128
SUBLANES = 8
VMEM_LIMIT_BYTES = 56 * 1024 * 1024
RESIDENT_WEIGHT_BYTES = 16 * 1024 * 1024
FAST_PATH_LIMIT = 1e30
FFN_ROW_CHUNK = 128
FFN_COL_TILE = 512
NORM_ROW_CHUNK = 256

F32 = jnp.float32
BF16 = jnp.bfloat16


def _pick(n, prefs):
    for p in prefs:
        if n % p == 0:
            return p
    return n


def _params(*sem):
    return pltpu.CompilerParams(dimension_semantics=sem, vmem_limit_bytes=VMEM_LIMIT_BYTES)


def _silu(x):
    return x / (1.0 + jnp.exp(-x))


def _mod_kernel(c_ref, w_ref, b_ref, o_ref):
    ca = _silu(c_ref[...]).astype(BF16)
    o_ref[0] = jnp.dot(ca, w_ref[0].astype(BF16), preferred_element_type=F32) + b_ref[0]


def _modulation(c, mod_w, mod_b):
    depth, d, n = mod_w.shape
    b = c.shape[0]
    tn = _pick(n, (1024, 512, 256, 128))
    return pl.pallas_call(
        _mod_kernel,
        grid=(depth, n // tn),
        in_specs=[
            pl.BlockSpec((b, d), lambda l, j: (0, 0)),
            pl.BlockSpec((1, d, tn), lambda l, j: (l, 0, j)),
            pl.BlockSpec((1, 1, tn), lambda l, j: (l, 0, j)),
        ],
        out_specs=pl.BlockSpec((1, b, tn), lambda l, j: (l, 0, j)),
        out_shape=jax.ShapeDtypeStruct((depth, b, n), F32),
        compiler_params=_params("arbitrary", "arbitrary"),
        name="adaln_modulation",
    )(c, mod_w, mod_b.reshape(depth, 1, n))


def _norm_modulate(x_ref, g_ref, sc_ref, sh_ref, h_ref, rows=slice(None)):
    x = x_ref[rows, :]
    r = lax.rsqrt(jnp.mean(x * x, axis=-1, keepdims=True) + EPS)
    a = g_ref[...] * (1.0 + sc_ref[0])
    h_ref[rows, :] = ((x * r) * a + sh_ref[0]).astype(BF16)


def _row_chunks(tm):
    rc = min(tm, NORM_ROW_CHUNK)
    return [slice(r0, r0 + rc) for r0 in range(0, tm, rc)]


def _row_specs(tm, d, rows_per_batch):
    tiles_per_batch = rows_per_batch // tm
    return [
        pl.BlockSpec((tm, d), lambda i, j: (i, 0)),
        pl.BlockSpec((1, d), lambda i, j: (0, 0)),
        pl.BlockSpec((1, 1, d), lambda i, j: (i // tiles_per_batch, 0, 0)),
        pl.BlockSpec((1, 1, d), lambda i, j: (i // tiles_per_batch, 0, 0)),
    ]


def _attn_in_kernel(x_ref, g_ref, sc_ref, sh_ref, w_ref, wf_ref, bf_ref, qkv_ref, fl_ref, h_ref,
                    *, q_tiles, q_scale):
    j = pl.program_id(1)
    scale = jnp.where(j < q_tiles, q_scale, 1.0)

    @pl.when(j == 0)
    def _():
        for rows in _row_chunks(x_ref.shape[0]):
            _norm_modulate(x_ref, g_ref, sc_ref, sh_ref, h_ref, rows)
            h = h_ref[rows, :]
            fl_ref[rows, :] = jnp.dot(h, wf_ref[...], preferred_element_type=F32) + bf_ref[...]
            qkv_ref[rows, :] = (jnp.dot(h, w_ref[...], preferred_element_type=F32) * scale).astype(BF16)

    @pl.when(j != 0)
    def _():
        acc = jnp.dot(h_ref[...], w_ref[...], preferred_element_type=F32)
        qkv_ref[...] = (acc * scale).astype(BF16)


def _attn_in_proj(x2, g, sc, sh, w_in_all, layer, w_f, b_f, rows_per_batch):
    m, d = x2.shape
    n = 3 * d
    tm = _pick(rows_per_batch, (1024, 512, 256, 128))
    tn = _pick(d, (2048, 1024, 512, 256, 128))
    kern = functools.partial(_attn_in_kernel, q_tiles=d // tn, q_scale=HEAD_DIM ** -0.5 * LOG2E)
    return pl.pallas_call(
        kern,
        grid=(m // tm, n // tn),
        in_specs=_row_specs(tm, d, rows_per_batch) + [
            pl.BlockSpec((None, d, tn), lambda i, j: (layer, 0, j)),
            pl.BlockSpec((d, LANES), lambda i, j: (0, 0)),
            pl.BlockSpec((1, LANES), lambda i, j: (0, 0)),
        ],
        out_specs=[
            pl.BlockSpec((tm, tn), lambda i, j: (i, j)),
            pl.BlockSpec((tm, LANES), lambda i, j: (i, 0)),
        ],
        out_shape=[
            jax.ShapeDtypeStruct((m, n), BF16),
            jax.ShapeDtypeStruct((m, LANES), F32),
        ],
        scratch_shapes=[pltpu.VMEM((tm, d), BF16)],
        compiler_params=_params("arbitrary", "arbitrary"),
        name="attn_in_proj",
    )(x2, g, sc, sh, w_in_all, w_f, b_f)


def _split3(x):
    hi = x.astype(BF16)
    r1 = x - hi.astype(F32)
    mid = r1.astype(BF16)
    lo = (r1 - mid.astype(F32)).astype(BF16)
    return hi, mid, lo


def _forget_cumsum_kernel(fl_ref, f_ref, ft_ref, carry_ref, *, sub):
    @pl.when(pl.program_id(1) == 0)
    def _():
        carry_ref[...] = jnp.zeros_like(carry_ref)

    tc = fl_ref.shape[0]
    row = lax.broadcasted_iota(jnp.int32, (sub, sub), 0)
    col = lax.broadcasted_iota(jnp.int32, (sub, sub), 1)
    tril = jnp.where(row >= col, 1.0, 0.0).astype(BF16)
    carry = carry_ref[0:1, :]
    for r in range(tc // sub):
        z = fl_ref[r * sub:(r + 1) * sub, :]
        lf = jnp.minimum(z, 0.0) - jnp.log1p(jnp.exp(-jnp.abs(z)))
        hi, mid, lo = _split3(lf)
        c = (jnp.dot(tril, lo, preferred_element_type=F32)
             + jnp.dot(tril, mid, preferred_element_type=F32)
             + jnp.dot(tril, hi, preferred_element_type=F32)) + carry
        c2 = c * LOG2E
        f_ref[r * sub:(r + 1) * sub, :] = c2
        ft_ref[0, :, r * sub:(r + 1) * sub] = c2.T
        carry = c[sub - 1:sub, :]
    carry_ref[0:1, :] = carry


def _forget_cumsum(flog, batch, seq):
    tc = _pick(seq, (1024, 512, 256, 128))
    sub = _pick(tc, (256, 128))
    nt = seq // tc
    return pl.pallas_call(
        functools.partial(_forget_cumsum_kernel, sub=sub),
        grid=(batch, nt),
        in_specs=[pl.BlockSpec((tc, LANES), lambda b, t: (b * nt + t, 0))],
        out_specs=[
            pl.BlockSpec((tc, LANES), lambda b, t: (b * nt + t, 0)),
            pl.BlockSpec((1, LANES, tc), lambda b, t: (b, 0, t)),
        ],
        out_shape=[
            jax.ShapeDtypeStruct((batch * seq, LANES), F32),
            jax.ShapeDtypeStruct((batch, LANES, seq), F32),
        ],
        scratch_shapes=[pltpu.VMEM((SUBLANES, LANES), F32)],
        compiler_params=_params("arbitrary", "arbitrary"),
        name="forget_cumsum",
    )(flog)


def _fox_attn_kernel(q_ref, k_ref, v_ref, f_ref, ft_ref, o_ref, vt_ref, fkb_ref, vmax_ref, m_ref, l_ref, acc_ref,
                     *, t, hg):
    qi = pl.program_id(2)
    nq = vt_ref.shape[1]
    heads = [pl.program_id(1) * hg + g for g in range(hg)]

    @pl.when(qi == 0)
    def _():
        lane = lax.broadcasted_iota(jnp.int32, (t, LANES), 1)
        for g in range(hg):
            v_abs = jnp.abs(v_ref[:, g * HEAD_DIM:(g + 1) * HEAD_DIM].astype(F32))
            vmax_ref[g] = jnp.max(v_abs, axis=0, keepdims=True)
            for c in range(nq):
                rows = slice(c * t, (c + 1) * t)
                vt_ref[g, c] = v_ref[rows, g * HEAD_DIM:(g + 1) * HEAD_DIM].T
                fk = jnp.sum(jnp.where(lane == heads[g], f_ref[rows, :], 0.0), axis=-1, keepdims=True)
                fkb_ref[g, rows, :] = jnp.broadcast_to(fk, (t, LANES))

    qs = [q_ref[:, g * HEAD_DIM:(g + 1) * HEAD_DIM] for g in range(hg)]
    fqs = [ft_ref[0, heads[g], pl.ds(qi, 1), :] for g in range(hg)]

    def scores(g, j):
        start = pl.multiple_of(j * t, t)
        kj = k_ref[pl.ds(start, t), g * HEAD_DIM:(g + 1) * HEAD_DIM]
        s = lax.dot_general(kj, qs[g], (((1,), (1,)), ((), ())), preferred_element_type=F32)
        return s - jnp.tile(fkb_ref[g, pl.ds(start, t), :], (1, t // LANES))

    def update(g, j, s, m_prev, l_prev, acc_prev):
        m_new = jnp.maximum(m_prev, jnp.max(s, axis=0, keepdims=True) + fqs[g])
        alpha = jnp.exp2(m_prev - m_new)
        p = jnp.exp2(s - (m_new - fqs[g]))
        l_new = alpha * l_prev + jnp.sum(p, axis=0, keepdims=True)
        pv = jnp.dot(vt_ref[g, j], p.astype(BF16), preferred_element_type=F32)
        return m_new, l_new, alpha * acc_prev + pv

    def load_state(g):
        return m_ref[g], l_ref[g], acc_ref[g]

    def store_state(g, state):
        m_ref[g], l_ref[g], acc_ref[g] = state

    def exact_body(it, _):
        j = qi - 1 - it
        for g in range(hg):
            store_state(g, update(g, j, scores(g, j), *load_state(g)))
        return 0

    def fast_body(it, _):
        j = qi - 1 - it
        s_next = scores(0, j)
        for g in range(hg):
            s_cur = s_next
            if g + 1 < hg:
                s_next = scores(g + 1, j)
            p = jnp.exp2(s_cur - (m_ref[g] - fqs[g]))
            l_ref[g] += jnp.sum(p, axis=0, keepdims=True)
            acc_ref[g] += jnp.dot(vt_ref[g, j], p.astype(BF16), preferred_element_type=F32)
        return 0

    half = t // 2
    causal = (lax.broadcasted_iota(jnp.int32, (half, half), 0) <= lax.broadcasted_iota(jnp.int32, (half, half), 1))

    def diag_scores(g):
        start = pl.multiple_of(qi * t, t)
        cols = slice(g * HEAD_DIM, (g + 1) * HEAD_DIM)
        nt = (((1,), (1,)), ((), ()))
        sa = lax.dot_general(k_ref[pl.ds(start, half), cols], qs[g], nt, preferred_element_type=F32)
        sa = sa - jnp.tile(fkb_ref[g, pl.ds(start, half), :], (1, t // LANES))
        sb = lax.dot_general(k_ref[pl.ds(start + half, half), cols], qs[g][half:, :], nt,
                             preferred_element_type=F32)
        sb = sb - jnp.tile(fkb_ref[g, pl.ds(start + half, half), :], (1, half // LANES))
        sa = jnp.concatenate([jnp.where(causal, sa[:, :half], -jnp.inf), sa[:, half:]], axis=1)
        return sa, jnp.where(causal, sb, -jnp.inf)

    def late(x_all, x_late, op):
        return jnp.concatenate([x_all[:, :half], op(x_all[:, half:], x_late)], axis=1)

    def diag_update(g, sa, sb):
        m_new = late(jnp.max(sa, axis=0, keepdims=True), jnp.max(sb, axis=0, keepdims=True), jnp.maximum) + fqs[g]
        shift = m_new - fqs[g]
        pa = jnp.exp2(sa - shift)
        pb = jnp.exp2(sb - shift[:, half:])
        l_new = late(jnp.sum(pa, axis=0, keepdims=True), jnp.sum(pb, axis=0, keepdims=True), jnp.add)
        vt = vt_ref[g, qi]
        acc_a = jnp.dot(vt[:, :half], pa.astype(BF16), preferred_element_type=F32)
        acc_b = jnp.dot(vt[:, half:], pb.astype(BF16), preferred_element_type=F32)
        return m_new, l_new, late(acc_a, acc_b, jnp.add)

    def diag_block():
        s_diag = [diag_scores(g) for g in range(hg)]
        for g in range(hg):
            store_state(g, diag_update(g, *s_diag[g]))

    diag_block()
    lax.fori_loop(0, qi, fast_body, 0)

    risk = jnp.zeros((1, t), F32)
    for g in range(hg):
        bound = l_ref[g] * jnp.maximum(jnp.max(vmax_ref[g], axis=-1, keepdims=True), 1.0)
        risk = jnp.maximum(risk, jnp.where(bound < FAST_PATH_LIMIT, 0.0, 1.0))

    @pl.when(jnp.max(risk) > 0.0)
    def _():
        diag_block()
        lax.fori_loop(0, qi, exact_body, 0)

    for g in range(hg):
        inv_l = 1.0 / l_ref[g]
        o_ref[:, g * HEAD_DIM:(g + 1) * HEAD_DIM] = (acc_ref[g] * inv_l).T.astype(BF16)


def _fox_attention(qkv, f, ft4, batch, seq, d):
    heads = d // HEAD_DIM
    hg = _pick(heads, (4, 2, 1))
    wg = hg * HEAD_DIM
    ng = heads // hg
    t = ft4.shape[-1]
    assert t % (2 * LANES) == 0, "the diagonal block is processed in lane-aligned halves"
    nq = seq // t
    return pl.pallas_call(
        functools.partial(_fox_attn_kernel, t=t, hg=hg),
        grid=(batch, ng, nq),
        in_specs=[
            pl.BlockSpec((t, wg), lambda b, h, i: (b * nq + i, h)),
            pl.BlockSpec((seq, wg), lambda b, h, i: (b, ng + h)),
            pl.BlockSpec((seq, wg), lambda b, h, i: (b, 2 * ng + h)),
            pl.BlockSpec((seq, LANES), lambda b, h, i: (b, 0)),
            pl.BlockSpec((1, heads, nq, t), lambda b, h, i: (b, 0, 0, 0)),
        ],
        out_specs=pl.BlockSpec((t, wg), lambda b, h, i: (b * nq + i, h)),
        out_shape=jax.ShapeDtypeStruct((batch * seq, d), BF16),
        scratch_shapes=[
            pltpu.VMEM((hg, nq, HEAD_DIM, t), BF16),
            pltpu.VMEM((hg, seq, LANES), F32),
            pltpu.VMEM((hg, 1, HEAD_DIM), F32),
            pltpu.VMEM((hg, 1, t), F32),
            pltpu.VMEM((hg, 1, t), F32),
            pltpu.VMEM((hg, HEAD_DIM, t), F32),
        ],
        compiler_params=_params("arbitrary", "arbitrary", "arbitrary"),
        name="fox_attention",
    )(qkv, qkv, qkv, f, ft4)


def _gelu_tanh(x):
    c = 0.7978845608028654
    return 0.5 * x * (1.0 + jnp.tanh(c * (x + 0.044715 * (x * x * x))))


def _gm_in_kernel(x_ref, g_ref, sc_ref, sh_ref, w_ref, z_ref, h_ref):
    j = pl.program_id(1)

    @pl.when(j == 0)
    def _():
        for rows in _row_chunks(x_ref.shape[0]):
            _norm_modulate(x_ref, g_ref, sc_ref, sh_ref, h_ref, rows)
            acc = jnp.dot(h_ref[rows, :], w_ref[...], preferred_element_type=F32)
            z_ref[rows, :] = _gelu_tanh(acc).astype(BF16)

    @pl.when(j != 0)
    def _():
        acc = jnp.dot(h_ref[...], w_ref[...], preferred_element_type=F32)
        z_ref[...] = _gelu_tanh(acc).astype(BF16)


def _gm_in_proj(x2, g, sc, sh, w_all, layer, rows_per_batch):
    m, d = x2.shape
    n = w_all.shape[2]
    tm = _pick(rows_per_batch, (1024, 512, 256, 128))
    tn = _pick(n, (2048, 1024, 512, 256, 128))
    return pl.pallas_call(
        _gm_in_kernel,
        grid=(m // tm, n // tn),
        in_specs=_row_specs(tm, d, rows_per_batch) + [
            pl.BlockSpec((None, d, tn), lambda i, j: (layer, 0, j))],
        out_specs=pl.BlockSpec((tm, tn), lambda i, j: (i, j)),
        out_shape=jax.ShapeDtypeStruct((m, n), BF16),
        scratch_shapes=[pltpu.VMEM((tm, d), BF16)],
        compiler_params=_params("arbitrary", "arbitrary"),
        name="gmlp_in_proj",
    )(x2, g, sc, sh, w_all)


def _gm_gate_kernel(u_ref, v_ref, vg_ref, ws_ref, bst_ref, o_ref):
    rows, dg = v_ref.shape
    groups = dg // GM_GROUP
    v = v_ref[...].astype(F32)
    r = lax.rsqrt(jnp.mean(v * v, axis=-1, keepdims=True) + EPS)
    vn = ((v * r) * vg_ref[...]).astype(BF16)
    row = lax.broadcasted_iota(jnp.int32, (CHUNK, CHUNK), 0)
    col = lax.broadcasted_iota(jnp.int32, (CHUNK, CHUNK), 1)
    causal = col <= row
    for g in range(groups):
        w = jnp.where(causal, ws_ref[g], 0.0).astype(BF16)
        bias = bst_ref[:, g:g + 1]
        lo = g * GM_GROUP
        for c in range(rows // CHUNK):
            r0 = c * CHUNK
            sv = jnp.dot(w, vn[r0:r0 + CHUNK, lo:lo + GM_GROUP], preferred_element_type=F32) + bias
            u = u_ref[r0:r0 + CHUNK, lo:lo + GM_GROUP].astype(F32)
            o_ref[r0:r0 + CHUNK, lo:lo + GM_GROUP] = (u * sv).astype(BF16)


def _gm_gate(z, v_g, w_s, b_s_t, rows_per_batch):
    m, n2 = z.shape
    dg = n2 // 2
    groups = dg // GM_GROUP
    tr = _pick(rows_per_batch, (512, 256, 128))
    return pl.pallas_call(
        _gm_gate_kernel,
        grid=(m // tr,),
        in_specs=[
            pl.BlockSpec((tr, dg), lambda i: (i, 0)),
            pl.BlockSpec((tr, dg), lambda i: (i, 1)),
            pl.BlockSpec((1, dg), lambda i: (0, 0)),
            pl.BlockSpec((groups, CHUNK, CHUNK), lambda i: (0, 0, 0)),
            pl.BlockSpec((CHUNK, groups), lambda i: (0, 0)),
        ],
        out_specs=pl.BlockSpec((tr, dg), lambda i: (i, 0)),
        out_shape=jax.ShapeDtypeStruct((m, dg), BF16),
        compiler_params=_params("arbitrary"),
        name="gmlp_spatial_gate",
    )(z, z, v_g, w_s, b_s_t)


def _ffn_in_kernel(x_ref, g_ref, sc_ref, sh_ref, wg_ref, wu_ref, cwg_ref, cwu_ref, cbg_ref, cbu_ref,
                   o_ref, h_ref, carry_g_ref, carry_u_ref, *, tiles_per_batch):
    i = pl.program_id(0)
    j = pl.program_id(1)
    tm = x_ref.shape[0]
    halo = SUBLANES
    rc = min(tm, FFN_ROW_CHUNK)

    @pl.when((i % tiles_per_batch) == 0)
    def _():
        carry_g_ref[j] = jnp.zeros(carry_g_ref.shape[1:], F32)
        carry_u_ref[j] = jnp.zeros(carry_u_ref.shape[1:], F32)

    sub = lax.broadcasted_iota(jnp.int32, (halo, wg_ref.shape[1]), 0)

    def shifted(acc, tail, k):
        moved = pltpu.roll(acc, k, 0)
        head = jnp.where(sub < k, pltpu.roll(tail, k, 0), moved[0:halo, :])
        return jnp.concatenate([head, moved[halo:, :]], axis=0)

    def conv(acc, tail, cw_ref, cb_ref):
        return (cw_ref[0:1, :] * shifted(acc, tail, 2) + cw_ref[1:2, :] * shifted(acc, tail, 1)
                + cw_ref[2:3, :] * acc + cb_ref[...])

    sizes = [rc] * (tm // rc)
    if rc // 2 >= 128 and rc % 16 == 0:
        sizes = sizes[:-1] + [rc // 2, rc // 2]

    def sweep(normalize):
        tail_g = carry_g_ref[j]
        tail_u = carry_u_ref[j]
        r0 = 0
        for n_rows in sizes:
            rows = slice(r0, r0 + n_rows)
            if normalize:
                _norm_modulate(x_ref, g_ref, sc_ref, sh_ref, h_ref, rows)
            h = h_ref[rows, :]
            acc_g = jnp.dot(h, wg_ref[...], preferred_element_type=F32)
            acc_u = jnp.dot(h, wu_ref[...], preferred_element_type=F32)
            gate = conv(acc_g, tail_g, cwg_ref, cbg_ref)
            up = conv(acc_u, tail_u, cwu_ref, cbu_ref)
            o_ref[rows, :] = (_silu(gate) * up).astype(BF16)
            tail_g = acc_g[n_rows - halo:n_rows, :]
            tail_u = acc_u[n_rows - halo:n_rows, :]
            r0 += n_rows
        carry_g_ref[j] = tail_g
        carry_u_ref[j] = tail_u

    @pl.when(j == 0)
    def _():
        sweep(True)

    @pl.when(j != 0)
    def _():
        sweep(False)


def _ffn_in(x2, g, sc, sh, wg_all, wu_all, layer, cwg, cwu, cbg, cbu, rows_per_batch):
    m, d = x2.shape
    fp = wg_all.shape[2]
    tm = _pick(rows_per_batch, (1024, 512, 256, 128))
    tn = FFN_COL_TILE
    nj = fp // tn
    kern = functools.partial(_ffn_in_kernel, tiles_per_batch=rows_per_batch // tm)
    col = lambda i, j: (0, j)
    wcol = lambda i, j: (layer, 0, j)
    return pl.pallas_call(
        kern,
        grid=(m // tm, nj),
        in_specs=_row_specs(tm, d, rows_per_batch) + [
            pl.BlockSpec((None, d, tn), wcol),
            pl.BlockSpec((None, d, tn), wcol),
            pl.BlockSpec((CONV_W, tn), col),
            pl.BlockSpec((CONV_W, tn), col),
            pl.BlockSpec((1, tn), col),
            pl.BlockSpec((1, tn), col),
        ],
        out_specs=pl.BlockSpec((tm, tn), lambda i, j: (i, j)),
        out_shape=jax.ShapeDtypeStruct((m, fp), BF16),
        scratch_shapes=[
            pltpu.VMEM((tm, d), BF16),
            pltpu.VMEM((nj, SUBLANES, tn), F32),
            pltpu.VMEM((nj, SUBLANES, tn), F32),
        ],
        compiler_params=_params("arbitrary", "arbitrary"),
        name="ffn_in_conv_gate",
    )(x2, g, sc, sh, wg_all, wu_all, cwg, cwu, cbg, cbu)


def _out_proj_kernel(a_ref, w_ref, x_ref, gate_ref, o_ref):
    y = jnp.dot(a_ref[...], w_ref[...], preferred_element_type=F32)
    o_ref[...] = x_ref[...] + gate_ref[0] * y


def _out_proj_residual(a, w_all, layer, x2, gate, rows_per_batch):
    m = a.shape[0]
    k, n = w_all.shape[1:]
    assert k % LANES == 0 and k <= a.shape[1]
    tm = _pick(rows_per_batch, (1024, 512, 256, 128))
    tn = _pick(n, (1024, 512, 256, 128))
    if k > 2 * n:
        tn = _pick(n, (512, 256, 128))
    elif 2 * k * n * 2 <= RESIDENT_WEIGHT_BYTES:
        tn = n
        tm = _pick(rows_per_batch, (512, 256, 128))
    tiles_per_batch = rows_per_batch // tm
    return pl.pallas_call(
        _out_proj_kernel,
        grid=(m // tm, n // tn),
        in_specs=[
            pl.BlockSpec((tm, k), lambda i, j: (i, 0)),
            pl.BlockSpec((None, k, tn), lambda i, j: (layer, 0, j)),
            pl.BlockSpec((tm, tn), lambda i, j: (i, j)),
            pl.BlockSpec((1, 1, tn), lambda i, j: (i // tiles_per_batch, 0, j)),
        ],
        out_specs=pl.BlockSpec((tm, tn), lambda i, j: (i, j)),
        out_shape=jax.ShapeDtypeStruct((m, n), F32),
        compiler_params=_params("arbitrary", "arbitrary"),
        name="out_proj_residual",
    )(a, w_all, x2, gate)


def _final_norm_kernel(x_ref, g_ref, o_ref):
    x = x_ref[...]
    r = lax.rsqrt(jnp.mean(x * x, axis=-1, keepdims=True) + EPS)
    o_ref[...] = (x * r) * g_ref[...]


def _final_norm(x2, g):
    m, d = x2.shape
    tm = _pick(m, (1024, 512, 256, 128))
    return pl.pallas_call(
        _final_norm_kernel,
        grid=(m // tm,),
        in_specs=[pl.BlockSpec((tm, d), lambda i: (i, 0)), pl.BlockSpec((1, d), lambda i: (0, 0))],
        out_specs=pl.BlockSpec((tm, d), lambda i: (i, 0)),
        out_shape=jax.ShapeDtypeStruct((m, d), F32),
        compiler_params=_params("arbitrary"),
        name="final_rms_norm",
    )(x2, g)


def _pad_cols(a, n):
    return jnp.pad(a, ((0, 0), (0, n - a.shape[1])))


def kernel(x, c, mod_w, mod_b, mix_norm_g, ffn_norm_g, attn_w_in, attn_b_f, attn_w_o, gm_w_in, gm_v_g, gm_w_s,
           gm_b_s, gm_w_o, ffn_w_in, ffn_conv_w, ffn_conv_b, ffn_w_out, final_g):
    batch, seq, d = x.shape
    depth = mod_w.shape[0]
    heads = d // HEAD_DIM
    d_ff = ffn_w_in.shape[-1] // 2
    fp = -(-d_ff // FFN_COL_TILE) * FFN_COL_TILE
    m = batch * seq
    assert d % LANES == 0 and seq % CHUNK == 0 and heads <= LANES
    t_attn = _pick(seq, (512, 256, 128))

    x2 = x.reshape(m, d)
    mod = _modulation(c, mod_w, mod_b).reshape(depth, batch, 6, 1, d)

    pad_ff = ((0, 0), (0, 0), (0, fp - d_ff))
    attn_w_in_b = attn_w_in.astype(BF16)
    attn_w_o_b = attn_w_o.astype(BF16)
    gm_w_in_b = gm_w_in.astype(BF16)
    gm_w_o_b = gm_w_o.astype(BF16)
    ffn_wg_b = jnp.pad(ffn_w_in[:, :, :d_ff], pad_ff).astype(BF16)
    ffn_wu_b = jnp.pad(ffn_w_in[:, :, d_ff:], pad_ff).astype(BF16)
    ffn_w_out_b = ffn_w_out.astype(BF16)

    for i in range(depth):
        sh1, sc1, g1, sh2, sc2, g2 = (mod[i, :, k] for k in range(6))
        j = i // 2
        gm = mix_norm_g[i].reshape(1, d)
        if i % 2 == 0:
            w_f = _pad_cols(attn_w_in[j][:, 3 * d:], LANES).astype(BF16)
            b_f = _pad_cols(attn_b_f[j].reshape(1, heads), LANES)
            qkv, flog = _attn_in_proj(x2, gm, sc1, sh1, attn_w_in_b, j, w_f, b_f, seq)
            f, ft = _forget_cumsum(flog, batch, seq)
            ft4 = ft[:, :heads, :].reshape(batch, heads, seq // t_attn, t_attn)
            y = _fox_attention(qkv, f, ft4, batch, seq, d)
            x2 = _out_proj_residual(y, attn_w_o_b, j, x2, g1, seq)
        else:
            z = _gm_in_proj(x2, gm, sc1, sh1, gm_w_in_b, j, seq)
            y = _gm_gate(z, gm_v_g[j].reshape(1, -1), gm_w_s[j], gm_b_s[j].T, seq)
            x2 = _out_proj_residual(y, gm_w_o_b, j, x2, g1, seq)

        cw = ffn_conv_w[i]
        cb = ffn_conv_b[i].reshape(1, 2 * d_ff)
        a = _ffn_in(x2, ffn_norm_g[i].reshape(1, d), sc2, sh2, ffn_wg_b, ffn_wu_b, i,
                    _pad_cols(cw[:, :d_ff], fp), _pad_cols(cw[:, d_ff:], fp),
                    _pad_cols(cb[:, :d_ff], fp), _pad_cols(cb[:, d_ff:], fp), seq)
        x2 = _out_proj_residual(a, ffn_w_out_b, i, x2, g2, seq)

    return _final_norm(x2, final_g.reshape(1, d)).reshape(batch, seq, d)
```

```python
import functools

import jax
import jax.numpy as jnp
from jax import lax
from jax.experimental import pallas as pl
from jax.experimental.pallas import tpu as pltpu

HEAD_DIM = 128
CHUNK = 128
GM_GROUP = 128
CONV_W = 3
EPS = 1e-6
LOG2E = 1.4426950408889634
LANES = 128
SUBLANES = 8
VMEM_LIMIT_BYTES = 56 * 1024 * 1024
RESIDENT_WEIGHT_BYTES = 16 * 1024 * 1024
FAST_PATH_LIMIT = 1e30
FFN_ROW_CHUNK = 128
FFN_COL_TILE = 512
NORM_ROW_CHUNK = 256

F32 = jnp.float32
BF16 = jnp.bfloat16


def _pick(n, prefs):
    for p in prefs:
        if n % p == 0:
            return p
    return n


def _params(*sem):
    return pltpu.CompilerParams(dimension_semantics=sem, vmem_limit_bytes=VMEM_LIMIT_BYTES)


def _silu(x):
    return x / (1.0 + jnp.exp(-x))


def _mod_kernel(c_ref, w_ref, b_ref, o_ref):
    ca = _silu(c_ref[...]).astype(BF16)
    o_ref[0] = jnp.dot(ca, w_ref[0].astype(BF16), preferred_element_type=F32) + b_ref[0]


def _modulation(c, mod_w, mod_b):
    depth, d, n = mod_w.shape
    b = c.shape[0]
    tn = _pick(n, (1024, 512, 256, 128))
    return pl.pallas_call(
        _mod_kernel,
        grid=(depth, n // tn),
        in_specs=[
            pl.BlockSpec((b, d), lambda l, j: (0, 0)),
            pl.BlockSpec((1, d, tn), lambda l, j: (l, 0, j)),
            pl.BlockSpec((1, 1, tn), lambda l, j: (l, 0, j)),
        ],
        out_specs=pl.BlockSpec((1, b, tn), lambda l, j: (l, 0, j)),
        out_shape=jax.ShapeDtypeStruct((depth, b, n), F32),
        compiler_params=_params("arbitrary", "arbitrary"),
        name="adaln_modulation",
    )(c, mod_w, mod_b.reshape(depth, 1, n))


def _norm_modulate(x_ref, g_ref, sc_ref, sh_ref, h_ref, rows=slice(None)):
    x = x_ref[rows, :]
    r = lax.rsqrt(jnp.mean(x * x, axis=-1, keepdims=True) + EPS)
    a = g_ref[...] * (1.0 + sc_ref[0])
    h_ref[rows, :] = ((x * r) * a + sh_ref[0]).astype(BF16)


def _row_chunks(tm):
    rc = min(tm, NORM_ROW_CHUNK)
    return [slice(r0, r0 + rc) for r0 in range(0, tm, rc)]


def _row_specs(tm, d, rows_per_batch):
    tiles_per_batch = rows_per_batch // tm
    return [
        pl.BlockSpec((tm, d), lambda i, j: (i, 0)),
        pl.BlockSpec((1, d), lambda i, j: (0, 0)),
        pl.BlockSpec((1, 1, d), lambda i, j: (i // tiles_per_batch, 0, 0)),
        pl.BlockSpec((1, 1, d), lambda i, j: (i // tiles_per_batch, 0, 0)),
    ]


def _attn_in_kernel(x_ref, g_ref, sc_ref, sh_ref, w_ref, wf_ref, bf_ref, qkv_ref, fl_ref, h_ref,
                    *, q_tiles, q_scale):
    j = pl.program_id(1)
    scale = jnp.where(j < q_tiles, q_scale, 1.0)

    @pl.when(j == 0)
    def _():
        for rows in _row_chunks(x_ref.shape[0]):
            _norm_modulate(x_ref, g_ref, sc_ref, sh_ref, h_ref, rows)
            h = h_ref[rows, :]
            fl_ref[rows, :] = jnp.dot(h, wf_ref[...], preferred_element_type=F32) + bf_ref[...]
            qkv_ref[rows, :] = (jnp.dot(h, w_ref[...], preferred_element_type=F32) * scale).astype(BF16)

    @pl.when(j != 0)
    def _():
        acc = jnp.dot(h_ref[...], w_ref[...], preferred_element_type=F32)
        qkv_ref[...] = (acc * scale).astype(BF16)


def _attn_in_proj(x2, g, sc, sh, w_in_all, layer, w_f, b_f, rows_per_batch):
    m, d = x2.shape
    n = 3 * d
    tm = _pick(rows_per_batch, (1024, 512, 256, 128))
    tn = _pick(d, (2048, 1024, 512, 256, 128))
    kern = functools.partial(_attn_in_kernel, q_tiles=d // tn, q_scale=HEAD_DIM ** -0.5 * LOG2E)
    return pl.pallas_call(
        kern,
        grid=(m // tm, n // tn),
        in_specs=_row_specs(tm, d, rows_per_batch) + [
            pl.BlockSpec((None, d, tn), lambda i, j: (layer, 0, j)),
            pl.BlockSpec((d, LANES), lambda i, j: (0, 0)),
            pl.BlockSpec((1, LANES), lambda i, j: (0, 0)),
        ],
        out_specs=[
            pl.BlockSpec((tm, tn), lambda i, j: (i, j)),
            pl.BlockSpec((tm, LANES), lambda i, j: (i, 0)),
        ],
        out_shape=[
            jax.ShapeDtypeStruct((m, n), BF16),
            jax.ShapeDtypeStruct((m, LANES), F32),
        ],
        scratch_shapes=[pltpu.VMEM((tm, d), BF16)],
        compiler_params=_params("arbitrary", "arbitrary"),
        name="attn_in_proj",
    )(x2, g, sc, sh, w_in_all, w_f, b_f)


def _split3(x):
    hi = x.astype(BF16)
    r1 = x - hi.astype(F32)
    mid = r1.astype(BF16)
    lo = (r1 - mid.astype(F32)).astype(BF16)
    return hi, mid, lo


def _forget_cumsum_kernel(fl_ref, f_ref, ft_ref, carry_ref, *, sub):
    @pl.when(pl.program_id(1) == 0)
    def _():
        carry_ref[...] = jnp.zeros_like(carry_ref)

    tc = fl_ref.shape[0]
    row = lax.broadcasted_iota(jnp.int32, (sub, sub), 0)
    col = lax.broadcasted_iota(jnp.int32, (sub, sub), 1)
    tril = jnp.where(row >= col, 1.0, 0.0).astype(BF16)
    carry = carry_ref[0:1, :]
    for r in range(tc // sub):
        z = fl_ref[r * sub:(r + 1) * sub, :]
        lf = jnp.minimum(z, 0.0) - jnp.log1p(jnp.exp(-jnp.abs(z)))
        hi, mid, lo = _split3(lf)
        c = (jnp.dot(tril, lo, preferred_element_type=F32)
             + jnp.dot(tril, mid, preferred_element_type=F32)
             + jnp.dot(tril, hi, preferred_element_type=F32)) + carry
        c2 = c * LOG2E
        f_ref[r * sub:(r + 1) * sub, :] = c2
        ft_ref[0, :, r * sub:(r + 1) * sub] = c2.T
        carry = c[sub - 1:sub, :]
    carry_ref[0:1, :] = carry


def _forget_cumsum(flog, batch, seq):
    tc = _pick(seq, (1024, 512, 256, 128))
    sub = _pick(tc, (256, 128))
    nt = seq // tc
    return pl.pallas_call(
        functools.partial(_forget_cumsum_kernel, sub=sub),
        grid=(batch, nt),
        in_specs=[pl.BlockSpec((tc, LANES), lambda b, t: (b * nt + t, 0))],
        out_specs=[
            pl.BlockSpec((tc, LANES), lambda b, t: (b * nt + t, 0)),
            pl.BlockSpec((1, LANES, tc), lambda b, t: (b, 0, t)),
        ],
        out_shape=[
            jax.ShapeDtypeStruct((batch * seq, LANES), F32),
            jax.ShapeDtypeStruct((batch, LANES, seq), F32),
        ],
        scratch_shapes=[pltpu.VMEM((SUBLANES, LANES), F32)],
        compiler_params=_params("arbitrary", "arbitrary"),
        name="forget_cumsum",
    )(flog)


def _fox_attn_kernel(q_ref, k_ref, v_ref, f_ref, ft_ref, o_ref, vt_ref, fkb_ref, vmax_ref, m_ref, l_ref, acc_ref,
                     *, t, hg):
    qi = pl.program_id(2)
    nq = vt_ref.shape[1]
    heads = [pl.program_id(1) * hg + g for g in range(hg)]

    @pl.when(qi == 0)
    def _():
        lane = lax.broadcasted_iota(jnp.int32, (t, LANES), 1)
        for g in range(hg):
            v_abs = jnp.abs(v_ref[:, g * HEAD_DIM:(g + 1) * HEAD_DIM].astype(F32))
            vmax_ref[g] = jnp.max(v_abs, axis=0, keepdims=True)
            for c in range(nq):
                rows = slice(c * t, (c + 1) * t)
                vt_ref[g, c] = v_ref[rows, g * HEAD_DIM:(g + 1) * HEAD_DIM].T
                fk = jnp.sum(jnp.where(lane == heads[g], f_ref[rows, :], 0.0), axis=-1, keepdims=True)
                fkb_ref[g, rows, :] = jnp.broadcast_to(fk, (t, LANES))

    qs = [q_ref[:, g * HEAD_DIM:(g + 1) * HEAD_DIM] for g in range(hg)]
    fqs = [ft_ref[0, heads[g], pl.ds(qi, 1), :] for g in range(hg)]

    def scores(g, j):
        start = pl.multiple_of(j * t, t)
        kj = k_ref[pl.ds(start, t), g * HEAD_DIM:(g + 1) * HEAD_DIM]
        s = lax.dot_general(kj, qs[g], (((1,), (1,)), ((), ())), preferred_element_type=F32)
        return s - jnp.tile(fkb_ref[g, pl.ds(start, t), :], (1, t // LANES))

    def update(g, j, s, m_prev, l_prev, acc_prev):
        m_new = jnp.maximum(m_prev, jnp.max(s, axis=0, keepdims=True) + fqs[g])
        alpha = jnp.exp2(m_prev - m_new)
        p = jnp.exp2(s - (m_new - fqs[g]))
        l_new = alpha * l_prev + jnp.sum(p, axis=0, keepdims=True)
        pv = jnp.dot(vt_ref[g, j], p.astype(BF16), preferred_element_type=F32)
        return m_new, l_new, alpha * acc_prev + pv

    def load_state(g):
        return m_ref[g], l_ref[g], acc_ref[g]

    def store_state(g, state):
        m_ref[g], l_ref[g], acc_ref[g] = state

    def exact_body(it, _):
        j = qi - 1 - it
        for g in range(hg):
            store_state(g, update(g, j, scores(g, j), *load_state(g)))
        return 0

    def fast_body(it, _):
        j = qi - 1 - it
        s_next = scores(0, j)
        for g in range(hg):
            s_cur = s_next
            if g + 1 < hg:
                s_next = scores(g + 1, j)
            p = jnp.exp2(s_cur - (m_ref[g] - fqs[g]))
            l_ref[g] += jnp.sum(p, axis=0, keepdims=True)
            acc_ref[g] += jnp.dot(vt_ref[g, j], p.astype(BF16), preferred_element_type=F32)
        return 0

    half = t // 2
    causal = (lax.broadcasted_iota(jnp.int32, (half, half), 0) <= lax.broadcasted_iota(jnp.int32, (half, half), 1))

    def diag_scores(g):
        start = pl.multiple_of(qi * t, t)
        cols = slice(g * HEAD_DIM, (g + 1) * HEAD_DIM)
        nt = (((1,), (1,)), ((), ()))
        sa = lax.dot_general(k_ref[pl.ds(start, half), cols], qs[g], nt, preferred_element_type=F32)
        sa = sa - jnp.tile(fkb_ref[g, pl.ds(start, half), :], (1, t // LANES))
        sb = lax.dot_general(k_ref[pl.ds(start + half, half), cols], qs[g][half:, :], nt,
                             preferred_element_type=F32)
        sb = sb - jnp.tile(fkb_ref[g, pl.ds(start + half, half), :], (1, half // LANES))
        sa = jnp.concatenate([jnp.where(causal, sa[:, :half], -jnp.inf), sa[:, half:]], axis=1)
        return sa, jnp.where(causal, sb, -jnp.inf)

    def late(x_all, x_late, op):
        return jnp.concatenate([x_all[:, :half], op(x_all[:, half:], x_late)], axis=1)

    def diag_update(g, sa, sb):
        m_new = late(jnp.max(sa, axis=0, keepdims=True), jnp.max(sb, axis=0, keepdims=True), jnp.maximum) + fqs[g]
        shift = m_new - fqs[g]
        pa = jnp.exp2(sa - shift)
        pb = jnp.exp2(sb - shift[:, half:])
        l_new = late(jnp.sum(pa, axis=0, keepdims=True), jnp.sum(pb, axis=0, keepdims=True), jnp.add)
        vt = vt_ref[g, qi]
        acc_a = jnp.dot(vt[:, :half], pa.astype(BF16), preferred_element_type=F32)
        acc_b = jnp.dot(vt[:, half:], pb.astype(BF16), preferred_element_type=F32)
        return m_new, l_new, late(acc_a, acc_b, jnp.add)

    def diag_block():
        s_diag = [diag_scores(g) for g in range(hg)]
        for g in range(hg):
            store_state(g, diag_update(g, *s_diag[g]))

    diag_block()
    lax.fori_loop(0, qi, fast_body, 0)

    risk = jnp.zeros((1, t), F32)
    for g in range(hg):
        bound = l_ref[g] * jnp.maximum(jnp.max(vmax_ref[g], axis=-1, keepdims=True), 1.0)
        risk = jnp.maximum(risk, jnp.where(bound < FAST_PATH_LIMIT, 0.0, 1.0))

    @pl.when(jnp.max(risk) > 0.0)
    def _():
        diag_block()
        lax.fori_loop(0, qi, exact_body, 0)

    for g in range(hg):
        inv_l = 1.0 / l_ref[g]
        o_ref[:, g * HEAD_DIM:(g + 1) * HEAD_DIM] = (acc_ref[g] * inv_l).T.astype(BF16)


def _fox_attention(qkv, f, ft4, batch, seq, d):
    heads = d // HEAD_DIM
    hg = _pick(heads, (4, 2, 1))
    wg = hg * HEAD_DIM
    ng = heads // hg
    t = ft4.shape[-1]
    assert t % (2 * LANES) == 0, "the diagonal block is processed in lane-aligned halves"
    nq = seq // t
    return pl.pallas_call(
        functools.partial(_fox_attn_kernel, t=t, hg=hg),
        grid=(batch, ng, nq),
        in_specs=[
            pl.BlockSpec((t, wg), lambda b, h, i: (b * nq + i, h)),
            pl.BlockSpec((seq, wg), lambda b, h, i: (b, ng + h)),
            pl.BlockSpec((seq, wg), lambda b, h, i: (b, 2 * ng + h)),
            pl.BlockSpec((seq, LANES), lambda b, h, i: (b, 0)),
            pl.BlockSpec((1, heads, nq, t), lambda b, h, i: (b, 0, 0, 0)),
        ],
        out_specs=pl.BlockSpec((t, wg), lambda b, h, i: (b * nq + i, h)),
        out_shape=jax.ShapeDtypeStruct((batch * seq, d), BF16),
        scratch_shapes=[
            pltpu.VMEM((hg, nq, HEAD_DIM, t), BF16),
            pltpu.VMEM((hg, seq, LANES), F32),
            pltpu.VMEM((hg, 1, HEAD_DIM), F32),
            pltpu.VMEM((hg, 1, t), F32),
            pltpu.VMEM((hg, 1, t), F32),
            pltpu.VMEM((hg, HEAD_DIM, t), F32),
        ],
        compiler_params=_params("arbitrary", "arbitrary", "arbitrary"),
        name="fox_attention",
    )(qkv, qkv, qkv, f, ft4)


def _gelu_tanh(x):
    c = 0.7978845608028654
    return 0.5 * x * (1.0 + jnp.tanh(c * (x + 0.044715 * (x * x * x))))


def _gm_in_kernel(x_ref, g_ref, sc_ref, sh_ref, w_ref, z_ref, h_ref):
    j = pl.program_id(1)

    @pl.when(j == 0)
    def _():
        for rows in _row_chunks(x_ref.shape[0]):
            _norm_modulate(x_ref, g_ref, sc_ref, sh_ref, h_ref, rows)
            acc = jnp.dot(h_ref[rows, :], w_ref[...], preferred_element_type=F32)
            z_ref[rows, :] = _gelu_tanh(acc).astype(BF16)

    @pl.when(j != 0)
    def _():
        acc = jnp.dot(h_ref[...], w_ref[...], preferred_element_type=F32)
        z_ref[...] = _gelu_tanh(acc).astype(BF16)


def _gm_in_proj(x2, g, sc, sh, w_all, layer, rows_per_batch):
    m, d = x2.shape
    n = w_all.shape[2]
    tm = _pick(rows_per_batch, (1024, 512, 256, 128))
    tn = _pick(n, (2048, 1024, 512, 256, 128))
    return pl.pallas_call(
        _gm_in_kernel,
        grid=(m // tm, n // tn),
        in_specs=_row_specs(tm, d, rows_per_batch) + [
            pl.BlockSpec((None, d, tn), lambda i, j: (layer, 0, j))],
        out_specs=pl.BlockSpec((tm, tn), lambda i, j: (i, j)),
        out_shape=jax.ShapeDtypeStruct((m, n), BF16),
        scratch_shapes=[pltpu.VMEM((tm, d), BF16)],
        compiler_params=_params("arbitrary", "arbitrary"),
        name="gmlp_in_proj",
    )(x2, g, sc, sh, w_all)


def _gm_gate_kernel(u_ref, v_ref, vg_ref, ws_ref, bst_ref, o_ref):
    rows, dg = v_ref.shape
    groups = dg // GM_GROUP
    v = v_ref[...].astype(F32)
    r = lax.rsqrt(jnp.mean(v * v, axis=-1, keepdims=True) + EPS)
    vn = ((v * r) * vg_ref[...]).astype(BF16)
    row = lax.broadcasted_iota(jnp.int32, (CHUNK, CHUNK), 0)
    col = lax.broadcasted_iota(jnp.int32, (CHUNK, CHUNK), 1)
    causal = col <= row
    for g in range(groups):
        w = jnp.where(causal, ws_ref[g], 0.0).astype(BF16)
        bias = bst_ref[:, g:g + 1]
        lo = g * GM_GROUP
        for c in range(rows // CHUNK):
            r0 = c * CHUNK
            sv = jnp.dot(w, vn[r0:r0 + CHUNK, lo:lo + GM_GROUP], preferred_element_type=F32) + bias
            u = u_ref[r0:r0 + CHUNK, lo:lo + GM_GROUP].astype(F32)
            o_ref[r0:r0 + CHUNK, lo:lo + GM_GROUP] = (u * sv).astype(BF16)


def _gm_gate(z, v_g, w_s, b_s_t, rows_per_batch):
    m, n2 = z.shape
    dg = n2 // 2
    groups = dg // GM_GROUP
    tr = _pick(rows_per_batch, (512, 256, 128))
    return pl.pallas_call(
        _gm_gate_kernel,
        grid=(m // tr,),
        in_specs=[
            pl.BlockSpec((tr, dg), lambda i: (i, 0)),
            pl.BlockSpec((tr, dg), lambda i: (i, 1)),
            pl.BlockSpec((1, dg), lambda i: (0, 0)),
            pl.BlockSpec((groups, CHUNK, CHUNK), lambda i: (0, 0, 0)),
            pl.BlockSpec((CHUNK, groups), lambda i: (0, 0)),
        ],
        out_specs=pl.BlockSpec((tr, dg), lambda i: (i, 0)),
        out_shape=jax.ShapeDtypeStruct((m, dg), BF16),
        compiler_params=_params("arbitrary"),
        name="gmlp_spatial_gate",
    )(z, z, v_g, w_s, b_s_t)


def _ffn_in_kernel(x_ref, g_ref, sc_ref, sh_ref, wg_ref, wu_ref, cwg_ref, cwu_ref, cbg_ref, cbu_ref,
                   o_ref, h_ref, carry_g_ref, carry_u_ref, *, tiles_per_batch):
    i = pl.program_id(0)
    j = pl.program_id(1)
    tm = x_ref.shape[0]
    halo = SUBLANES
    rc = min(tm, FFN_ROW_CHUNK)

    @pl.when((i % tiles_per_batch) == 0)
    def _():
        carry_g_ref[j] = jnp.zeros(carry_g_ref.shape[1:], F32)
        carry_u_ref[j] = jnp.zeros(carry_u_ref.shape[1:], F32)

    sub = lax.broadcasted_iota(jnp.int32, (halo, wg_ref.shape[1]), 0)

    def shifted(acc, tail, k):
        moved = pltpu.roll(acc, k, 0)
        head = jnp.where(sub < k, pltpu.roll(tail, k, 0), moved[0:halo, :])
        return jnp.concatenate([head, moved[halo:, :]], axis=0)

    def conv(acc, tail, cw_ref, cb_ref):
        return (cw_ref[0:1, :] * shifted(acc, tail, 2) + cw_ref[1:2, :] * shifted(acc, tail, 1)
                + cw_ref[2:3, :] * acc + cb_ref[...])

    sizes = [rc] * (tm // rc)

    def sweep(normalize):
        tail_g = carry_g_ref[j]
        tail_u = carry_u_ref[j]
        r0 = 0
        for n_rows in sizes:
            rows = slice(r0, r0 + n_rows)
            if normalize:
                _norm_modulate(x_ref, g_ref, sc_ref, sh_ref, h_ref, rows)
            h = h_ref[rows, :]
            acc_g = jnp.dot(h, wg_ref[...], preferred_element_type=F32)
            acc_u = jnp.dot(h, wu_ref[...], preferred_element_type=F32)
            gate = conv(acc_g, tail_g, cwg_ref, cbg_ref)
            up = conv(acc_u, tail_u, cwu_ref, cbu_ref)
            o_ref[rows, :] = (_silu(gate) * up).astype(BF16)
            tail_g = acc_g[n_rows - halo:n_rows, :]
            tail_u = acc_u[n_rows - halo:n_rows, :]
            r0 += n_rows
        carry_g_ref[j] = tail_g
        carry_u_ref[j] = tail_u

    @pl.when(j == 0)
    def _():
        sweep(True)

    @pl.when(j != 0)
    def _():
        sweep(False)


def _ffn_in(x2, g, sc, sh, wg_all, wu_all, layer, cwg, cwu, cbg, cbu, rows_per_batch):
    m, d = x2.shape
    fp = wg_all.shape[2]
    tm = _pick(rows_per_batch, (1024, 512, 256, 128))
    tn = FFN_COL_TILE
    nj = fp // tn
    kern = functools.partial(_ffn_in_kernel, tiles_per_batch=rows_per_batch // tm)
    col = lambda i, j: (0, j)
    wcol = lambda i, j: (layer, 0, j)
    return pl.pallas_call(
        kern,
        grid=(m // tm, nj),
        in_specs=_row_specs(tm, d, rows_per_batch) + [
            pl.BlockSpec((None, d, tn), wcol),
            pl.BlockSpec((None, d, tn), wcol),
            pl.BlockSpec((CONV_W, tn), col),
            pl.BlockSpec((CONV_W, tn), col),
            pl.BlockSpec((1, tn), col),
            pl.BlockSpec((1, tn), col),
        ],
        out_specs=pl.BlockSpec((tm, tn), lambda i, j: (i, j)),
        out_shape=jax.ShapeDtypeStruct((m, fp), BF16),
        scratch_shapes=[
            pltpu.VMEM((tm, d), BF16),
            pltpu.VMEM((nj, SUBLANES, tn), F32),
            pltpu.VMEM((nj, SUBLANES, tn), F32),
        ],
        compiler_params=_params("arbitrary", "arbitrary"),
        name="ffn_in_conv_gate",
    )(x2, g, sc, sh, wg_all, wu_all, cwg, cwu, cbg, cbu)


def _out_proj_kernel(a_ref, w_ref, x_ref, gate_ref, o_ref):
    y = jnp.dot(a_ref[...], w_ref[...], preferred_element_type=F32)
    o_ref[...] = x_ref[...] + gate_ref[0] * y


def _out_proj_residual(a, w_all, layer, x2, gate, rows_per_batch):
    m = a.shape[0]
    k, n = w_all.shape[1:]
    assert k % LANES == 0 and k <= a.shape[1]
    tm = _pick(rows_per_batch, (1024, 512, 256, 128))
    tn = _pick(n, (1024, 512, 256, 128))
    if k > 2 * n:
        tn = _pick(n, (512, 256, 128))
    elif 2 * k * n * 2 <= RESIDENT_WEIGHT_BYTES:
        tn = n
        tm = _pick(rows_per_batch, (512, 256, 128))
    tiles_per_batch = rows_per_batch // tm
    return pl.pallas_call(
        _out_proj_kernel,
        grid=(m // tm, n // tn),
        in_specs=[
            pl.BlockSpec((tm, k), lambda i, j: (i, 0)),
            pl.BlockSpec((None, k, tn), lambda i, j: (layer, 0, j)),
            pl.BlockSpec((tm, tn), lambda i, j: (i, j)),
            pl.BlockSpec((1, 1, tn), lambda i, j: (i // tiles_per_batch, 0, j)),
        ],
        out_specs=pl.BlockSpec((tm, tn), lambda i, j: (i, j)),
        out_shape=jax.ShapeDtypeStruct((m, n), F32),
        compiler_params=_params("arbitrary", "arbitrary"),
        name="out_proj_residual",
    )(a, w_all, x2, gate)


def _final_norm_kernel(x_ref, g_ref, o_ref):
    x = x_ref[...]
    r = lax.rsqrt(jnp.mean(x * x, axis=-1, keepdims=True) + EPS)
    o_ref[...] = (x * r) * g_ref[...]


def _final_norm(x2, g):
    m, d = x2.shape
    tm = _pick(m, (1024, 512, 256, 128))
    return pl.pallas_call(
        _final_norm_kernel,
        grid=(m // tm,),
        in_specs=[pl.BlockSpec((tm, d), lambda i: (i, 0)), pl.BlockSpec((1, d), lambda i: (0, 0))],
        out_specs=pl.BlockSpec((tm, d), lambda i: (i, 0)),
        out_shape=jax.ShapeDtypeStruct((m, d), F32),
        compiler_params=_params("arbitrary"),
        name="final_rms_norm",
    )(x2, g)


def _pad_cols(a, n):
    return jnp.pad(a, ((0, 0), (0, n - a.shape[1])))


def kernel(x, c, mod_w, mod_b, mix_norm_g, ffn_norm_g, attn_w_in, attn_b_f, attn_w_o, gm_w_in, gm_v_g, gm_w_s,
           gm_b_s, gm_w_o, ffn_w_in, ffn_conv_w, ffn_conv_b, ffn_w_out, final_g):
    batch, seq, d = x.shape
    depth = mod_w.shape[0]
    heads = d // HEAD_DIM
    d_ff = ffn_w_in.shape[-1] // 2
    fp = -(-d_ff // FFN_COL_TILE) * FFN_COL_TILE
    m = batch * seq
    assert d % LANES == 0 and seq % CHUNK == 0 and heads <= LANES
    t_attn = _pick(seq, (512, 256, 128))

    x2 = x.reshape(m, d)
    mod = _modulation(c, mod_w, mod_b).reshape(depth, batch, 6, 1, d)

    pad_ff = ((0, 0), (0, 0), (0, fp - d_ff))
    attn_w_in_b = attn_w_in.astype(BF16)
    attn_w_o_b = attn_w_o.astype(BF16)
    gm_w_in_b = gm_w_in.astype(BF16)
    gm_w_o_b = gm_w_o.astype(BF16)
    ffn_wg_b = jnp.pad(ffn_w_in[:, :, :d_ff], pad_ff).astype(BF16)
    ffn_wu_b = jnp.pad(ffn_w_in[:, :, d_ff:], pad_ff).astype(BF16)
    ffn_w_out_b = ffn_w_out.astype(BF16)

    for i in range(depth):
        sh1, sc1, g1, sh2, sc2, g2 = (mod[i, :, k] for k in range(6))
        j = i // 2
        gm = mix_norm_g[i].reshape(1, d)
        if i % 2 == 0:
            w_f = _pad_cols(attn_w_in[j][:, 3 * d:], LANES).astype(BF16)
            b_f = _pad_cols(attn_b_f[j].reshape(1, heads), LANES)
            qkv, flog = _attn_in_proj(x2, gm, sc1, sh1, attn_w_in_b, j, w_f, b_f, seq)
            f, ft = _forget_cumsum(flog, batch, seq)
            ft4 = ft[:, :heads, :].reshape(batch, heads, seq // t_attn, t_attn)
            y = _fox_attention(qkv, f, ft4, batch, seq, d)
            x2 = _out_proj_residual(y, attn_w_o_b, j, x2, g1, seq)
        else:
            z = _gm_in_proj(x2, gm, sc1, sh1, gm_w_in_b, j, seq)
            y = _gm_gate(z, gm_v_g[j].reshape(1, -1), gm_w_s[j], gm_b_s[j].T, seq)
            x2 = _out_proj_residual(y, gm_w_o_b, j, x2, g1, seq)

        cw = ffn_conv_w[i]
        cb = ffn_conv_b[i].reshape(1, 2 * d_ff)
        a = _ffn_in(x2, ffn_norm_g[i].reshape(1, d), sc2, sh2, ffn_wg_b, ffn_wu_b, i,
                    _pad_cols(cw[:, :d_ff], fp), _pad_cols(cw[:, d_ff:], fp),
                    _pad_cols(cb[:, :d_ff], fp), _pad_cols(cb[:, d_ff:], fp), seq)
        x2 = _out_proj_residual(a, ffn_w_out_b, i, x2, g2, seq)

    return _final_norm(x2, final_g.reshape(1, d)).reshape(batch, seq, d)
```

```python
import functools

import jax
import jax.numpy as jnp
from jax import lax
from jax.experimental import pallas as pl
from jax.experimental.pallas import tpu as pltpu

HEAD_DIM = 128
CHUNK = 128
GM_GROUP = 128
CONV_W = 3
EPS = 1e-6
LOG2E = 1.4426950408889634
LANES = 128
SUBLANES = 8
VMEM_LIMIT_BYTES = 56 * 1024 * 1024
RESIDENT_WEIGHT_BYTES = 16 * 1024 * 1024
FAST_PATH_LIMIT = 1e30
FFN_ROW_CHUNK = 128
FFN_EPILOGUE_COLS = 128
FFN_COL_TILE = 512
NORM_ROW_CHUNK = 256

F32 = jnp.float32
BF16 = jnp.bfloat16


def _pick(n, prefs):
    for p in prefs:
        if n % p == 0:
            return p
    return n


def _params(*sem):
    return pltpu.CompilerParams(dimension_semantics=sem, vmem_limit_bytes=VMEM_LIMIT_BYTES)


def _silu(x):
    return x / (1.0 + jnp.exp(-x))


def _mod_kernel(c_ref, w_ref, b_ref, o_ref):
    ca = _silu(c_ref[...]).astype(BF16)
    o_ref[0] = jnp.dot(ca, w_ref[0].astype(BF16), preferred_element_type=F32) + b_ref[0]


def _modulation(c, mod_w, mod_b):
    depth, d, n = mod_w.shape
    b = c.shape[0]
    tn = _pick(n, (1024, 512, 256, 128))
    return pl.pallas_call(
        _mod_kernel,
        grid=(depth, n // tn),
        in_specs=[
            pl.BlockSpec((b, d), lambda l, j: (0, 0)),
            pl.BlockSpec((1, d, tn), lambda l, j: (l, 0, j)),
            pl.BlockSpec((1, 1, tn), lambda l, j: (l, 0, j)),
        ],
        out_specs=pl.BlockSpec((1, b, tn), lambda l, j: (l, 0, j)),
        out_shape=jax.ShapeDtypeStruct((depth, b, n), F32),
        compiler_params=_params("arbitrary", "arbitrary"),
        name="adaln_modulation",
    )(c, mod_w, mod_b.reshape(depth, 1, n))


def _norm_modulate(x_ref, g_ref, sc_ref, sh_ref, h_ref, rows=slice(None)):
    x = x_ref[rows, :]
    r = lax.rsqrt(jnp.mean(x * x, axis=-1, keepdims=True) + EPS)
    a = g_ref[...] * (1.0 + sc_ref[0])
    h_ref[rows, :] = ((x * r) * a + sh_ref[0]).astype(BF16)


def _row_chunks(tm):
    rc = min(tm, NORM_ROW_CHUNK)
    return [slice(r0, r0 + rc) for r0 in range(0, tm, rc)]


def _row_specs(tm, d, rows_per_batch):
    tiles_per_batch = rows_per_batch // tm
    return [
        pl.BlockSpec((tm, d), lambda i, j: (i, 0)),
        pl.BlockSpec((1, d), lambda i, j: (0, 0)),
        pl.BlockSpec((1, 1, d), lambda i, j: (i // tiles_per_batch, 0, 0)),
        pl.BlockSpec((1, 1, d), lambda i, j: (i // tiles_per_batch, 0, 0)),
    ]


def _attn_in_kernel(x_ref, g_ref, sc_ref, sh_ref, w_ref, wf_ref, bf_ref, qkv_ref, fl_ref, h_ref,
                    *, q_tiles, q_scale):
    j = pl.program_id(1)
    scale = jnp.where(j < q_tiles, q_scale, 1.0)

    @pl.when(j == 0)
    def _():
        for rows in _row_chunks(x_ref.shape[0]):
            _norm_modulate(x_ref, g_ref, sc_ref, sh_ref, h_ref, rows)
            h = h_ref[rows, :]
            fl_ref[rows, :] = jnp.dot(h, wf_ref[...], preferred_element_type=F32) + bf_ref[...]
            qkv_ref[rows, :] = (jnp.dot(h, w_ref[...], preferred_element_type=F32) * scale).astype(BF16)

    @pl.when(j != 0)
    def _():
        acc = jnp.dot(h_ref[...], w_ref[...], preferred_element_type=F32)
        qkv_ref[...] = (acc * scale).astype(BF16)


def _attn_in_proj(x2, g, sc, sh, w_in_all, layer, w_f, b_f, rows_per_batch):
    m, d = x2.shape
    n = 3 * d
    tm = _pick(rows_per_batch, (1024, 512, 256, 128))
    tn = _pick(d, (2048, 1024, 512, 256, 128))
    kern = functools.partial(_attn_in_kernel, q_tiles=d // tn, q_scale=HEAD_DIM ** -0.5 * LOG2E)
    return pl.pallas_call(
        kern,
        grid=(m // tm, n // tn),
        in_specs=_row_specs(tm, d, rows_per_batch) + [
            pl.BlockSpec((None, d, tn), lambda i, j: (layer, 0, j)),
            pl.BlockSpec((d, LANES), lambda i, j: (0, 0)),
            pl.BlockSpec((1, LANES), lambda i, j: (0, 0)),
        ],
        out_specs=[
            pl.BlockSpec((tm, tn), lambda i, j: (i, j)),
            pl.BlockSpec((tm, LANES), lambda i, j: (i, 0)),
        ],
        out_shape=[
            jax.ShapeDtypeStruct((m, n), BF16),
            jax.ShapeDtypeStruct((m, LANES), F32),
        ],
        scratch_shapes=[pltpu.VMEM((tm, d), BF16)],
        compiler_params=_params("arbitrary", "arbitrary"),
        name="attn_in_proj",
    )(x2, g, sc, sh, w_in_all, w_f, b_f)


def _split3(x):
    hi = x.astype(BF16)
    r1 = x - hi.astype(F32)
    mid = r1.astype(BF16)
    lo = (r1 - mid.astype(F32)).astype(BF16)
    return hi, mid, lo


def _forget_cumsum_kernel(fl_ref, f_ref, ft_ref, carry_ref, *, sub):
    @pl.when(pl.program_id(1) == 0)
    def _():
        carry_ref[...] = jnp.zeros_like(carry_ref)

    tc = fl_ref.shape[0]
    row = lax.broadcasted_iota(jnp.int32, (sub, sub), 0)
    col = lax.broadcasted_iota(jnp.int32, (sub, sub), 1)
    tril = jnp.where(row >= col, 1.0, 0.0).astype(BF16)
    carry = carry_ref[0:1, :]
    for r in range(tc // sub):
        z = fl_ref[r * sub:(r + 1) * sub, :]
        lf = jnp.minimum(z, 0.0) - jnp.log1p(jnp.exp(-jnp.abs(z)))
        hi, mid, lo = _split3(lf)
        c = (jnp.dot(tril, lo, preferred_element_type=F32)
             + jnp.dot(tril, mid, preferred_element_type=F32)
             + jnp.dot(tril, hi, preferred_element_type=F32)) + carry
        c2 = c * LOG2E
        f_ref[r * sub:(r + 1) * sub, :] = c2
        ft_ref[0, :, r * sub:(r + 1) * sub] = c2.T
        carry = c[sub - 1:sub, :]
    carry_ref[0:1, :] = carry


def _forget_cumsum(flog, batch, seq):
    tc = _pick(seq, (1024, 512, 256, 128))
    sub = _pick(tc, (256, 128))
    nt = seq // tc
    return pl.pallas_call(
        functools.partial(_forget_cumsum_kernel, sub=sub),
        grid=(batch, nt),
        in_specs=[pl.BlockSpec((tc, LANES), lambda b, t: (b * nt + t, 0))],
        out_specs=[
            pl.BlockSpec((tc, LANES), lambda b, t: (b * nt + t, 0)),
            pl.BlockSpec((1, LANES, tc), lambda b, t: (b, 0, t)),
        ],
        out_shape=[
            jax.ShapeDtypeStruct((batch * seq, LANES), F32),
            jax.ShapeDtypeStruct((batch, LANES, seq), F32),
        ],
        scratch_shapes=[pltpu.VMEM((SUBLANES, LANES), F32)],
        compiler_params=_params("arbitrary", "arbitrary"),
        name="forget_cumsum",
    )(flog)


def _fox_attn_kernel(q_ref, k_ref, v_ref, f_ref, ft_ref, o_ref, vt_ref, fkb_ref, vmax_ref, m_ref, l_ref, acc_ref,
                     *, t, hg):
    qi = pl.program_id(2)
    nq = vt_ref.shape[1]
    heads = [pl.program_id(1) * hg + g for g in range(hg)]

    @pl.when(qi == 0)
    def _():
        lane = lax.broadcasted_iota(jnp.int32, (t, LANES), 1)
        for g in range(hg):
            v_abs = jnp.abs(v_ref[:, g * HEAD_DIM:(g + 1) * HEAD_DIM].astype(F32))
            vmax_ref[g] = jnp.max(v_abs, axis=0, keepdims=True)
            for c in range(nq):
                rows = slice(c * t, (c + 1) * t)
                vt_ref[g, c] = v_ref[rows, g * HEAD_DIM:(g + 1) * HEAD_DIM].T
                fk = jnp.sum(jnp.where(lane == heads[g], f_ref[rows, :], 0.0), axis=-1, keepdims=True)
                fkb_ref[g, rows, :] = jnp.broadcast_to(fk, (t, LANES))

    qs = [q_ref[:, g * HEAD_DIM:(g + 1) * HEAD_DIM] for g in range(hg)]
    fqs = [ft_ref[0, heads[g], pl.ds(qi, 1), :] for g in range(hg)]

    def scores(g, j):
        start = pl.multiple_of(j * t, t)
        kj = k_ref[pl.ds(start, t), g * HEAD_DIM:(g + 1) * HEAD_DIM]
        s = lax.dot_general(kj, qs[g], (((1,), (1,)), ((), ())), preferred_element_type=F32)
        return s - jnp.tile(fkb_ref[g, pl.ds(start, t), :], (1, t // LANES))

    def update(g, j, s, m_prev, l_prev, acc_prev):
        m_new = jnp.maximum(m_prev, jnp.max(s, axis=0, keepdims=True) + fqs[g])
        alpha = jnp.exp2(m_prev - m_new)
        p = jnp.exp2(s - (m_new - fqs[g]))
        l_new = alpha * l_prev + jnp.sum(p, axis=0, keepdims=True)
        pv = jnp.dot(vt_ref[g, j], p.astype(BF16), preferred_element_type=F32)
        return m_new, l_new, alpha * acc_prev + pv

    def load_state(g):
        return m_ref[g], l_ref[g], acc_ref[g]

    def store_state(g, state):
        m_ref[g], l_ref[g], acc_ref[g] = state

    def exact_body(it, _):
        j = qi - 1 - it
        for g in range(hg):
            store_state(g, update(g, j, scores(g, j), *load_state(g)))
        return 0

    def fast_body(it, _):
        j = qi - 1 - it
        s_next = scores(0, j)
        for g in range(hg):
            s_cur = s_next
            if g + 1 < hg:
                s_next = scores(g + 1, j)
            p = jnp.exp2(s_cur - (m_ref[g] - fqs[g]))
            l_ref[g] += jnp.sum(p, axis=0, keepdims=True)
            acc_ref[g] += jnp.dot(vt_ref[g, j], p.astype(BF16), preferred_element_type=F32)
        return 0

    half = t // 2
    causal = (lax.broadcasted_iota(jnp.int32, (half, half), 0) <= lax.broadcasted_iota(jnp.int32, (half, half), 1))

    def diag_scores(g):
        start = pl.multiple_of(qi * t, t)
        cols = slice(g * HEAD_DIM, (g + 1) * HEAD_DIM)
        nt = (((1,), (1,)), ((), ()))
        sa = lax.dot_general(k_ref[pl.ds(start, half), cols], qs[g], nt, preferred_element_type=F32)
        sa = sa - jnp.tile(fkb_ref[g, pl.ds(start, half), :], (1, t // LANES))
        sb = lax.dot_general(k_ref[pl.ds(start + half, half), cols], qs[g][half:, :], nt,
                             preferred_element_type=F32)
        sb = sb - jnp.tile(fkb_ref[g, pl.ds(start + half, half), :], (1, half // LANES))
        sa = jnp.concatenate([jnp.where(causal, sa[:, :half], -jnp.inf), sa[:, half:]], axis=1)
        return sa, jnp.where(causal, sb, -jnp.inf)

    def late(x_all, x_late, op):
        return jnp.concatenate([x_all[:, :half], op(x_all[:, half:], x_late)], axis=1)

    def diag_update(g, sa, sb):
        m_new = late(jnp.max(sa, axis=0, keepdims=True), jnp.max(sb, axis=0, keepdims=True), jnp.maximum) + fqs[g]
        shift = m_new - fqs[g]
        pa = jnp.exp2(sa - shift)
        pb = jnp.exp2(sb - shift[:, half:])
        l_new = late(jnp.sum(pa, axis=0, keepdims=True), jnp.sum(pb, axis=0, keepdims=True), jnp.add)
        vt = vt_ref[g, qi]
        acc_a = jnp.dot(vt[:, :half], pa.astype(BF16), preferred_element_type=F32)
        acc_b = jnp.dot(vt[:, half:], pb.astype(BF16), preferred_element_type=F32)
        return m_new, l_new, late(acc_a, acc_b, jnp.add)

    def diag_block():
        s_diag = [diag_scores(g) for g in range(hg)]
        for g in range(hg):
            store_state(g, diag_update(g, *s_diag[g]))

    diag_block()
    lax.fori_loop(0, qi, fast_body, 0)

    risk = jnp.zeros((1, t), F32)
    for g in range(hg):
        bound = l_ref[g] * jnp.maximum(jnp.max(vmax_ref[g], axis=-1, keepdims=True), 1.0)
        risk = jnp.maximum(risk, jnp.where(bound < FAST_PATH_LIMIT, 0.0, 1.0))

    @pl.when(jnp.max(risk) > 0.0)
    def _():
        diag_block()
        lax.fori_loop(0, qi, exact_body, 0)

    for g in range(hg):
        inv_l = 1.0 / l_ref[g]
        o_ref[:, g * HEAD_DIM:(g + 1) * HEAD_DIM] = (acc_ref[g] * inv_l).T.astype(BF16)


def _fox_attention(qkv, f, ft4, batch, seq, d):
    heads = d // HEAD_DIM
    hg = _pick(heads, (4, 2, 1))
    wg = hg * HEAD_DIM
    ng = heads // hg
    t = ft4.shape[-1]
    assert t % (2 * LANES) == 0, "the diagonal block is processed in lane-aligned halves"
    nq = seq // t
    return pl.pallas_call(
        functools.partial(_fox_attn_kernel, t=t, hg=hg),
        grid=(batch, ng, nq),
        in_specs=[
            pl.BlockSpec((t, wg), lambda b, h, i: (b * nq + i, h)),
            pl.BlockSpec((seq, wg), lambda b, h, i: (b, ng + h)),
            pl.BlockSpec((seq, wg), lambda b, h, i: (b, 2 * ng + h)),
            pl.BlockSpec((seq, LANES), lambda b, h, i: (b, 0)),
            pl.BlockSpec((1, heads, nq, t), lambda b, h, i: (b, 0, 0, 0)),
        ],
        out_specs=pl.BlockSpec((t, wg), lambda b, h, i: (b * nq + i, h)),
        out_shape=jax.ShapeDtypeStruct((batch * seq, d), BF16),
        scratch_shapes=[
            pltpu.VMEM((hg, nq, HEAD_DIM, t), BF16),
            pltpu.VMEM((hg, seq, LANES), F32),
            pltpu.VMEM((hg, 1, HEAD_DIM), F32),
            pltpu.VMEM((hg, 1, t), F32),
            pltpu.VMEM((hg, 1, t), F32),
            pltpu.VMEM((hg, HEAD_DIM, t), F32),
        ],
        compiler_params=_params("arbitrary", "arbitrary", "arbitrary"),
        name="fox_attention",
    )(qkv, qkv, qkv, f, ft4)


def _gelu_tanh(x):
    c = 0.7978845608028654
    return 0.5 * x * (1.0 + jnp.tanh(c * (x + 0.044715 * (x * x * x))))


def _gm_in_kernel(x_ref, g_ref, sc_ref, sh_ref, w_ref, z_ref, h_ref):
    j = pl.program_id(1)

    @pl.when(j == 0)
    def _():
        for rows in _row_chunks(x_ref.shape[0]):
            _norm_modulate(x_ref, g_ref, sc_ref, sh_ref, h_ref, rows)
            acc = jnp.dot(h_ref[rows, :], w_ref[...], preferred_element_type=F32)
            z_ref[rows, :] = _gelu_tanh(acc).astype(BF16)

    @pl.when(j != 0)
    def _():
        acc = jnp.dot(h_ref[...], w_ref[...], preferred_element_type=F32)
        z_ref[...] = _gelu_tanh(acc).astype(BF16)


def _gm_in_proj(x2, g, sc, sh, w_all, layer, rows_per_batch):
    m, d = x2.shape
    n = w_all.shape[2]
    tm = _pick(rows_per_batch, (1024, 512, 256, 128))
    tn = _pick(n, (2048, 1024, 512, 256, 128))
    return pl.pallas_call(
        _gm_in_kernel,
        grid=(m // tm, n // tn),
        in_specs=_row_specs(tm, d, rows_per_batch) + [
            pl.BlockSpec((None, d, tn), lambda i, j: (layer, 0, j))],
        out_specs=pl.BlockSpec((tm, tn), lambda i, j: (i, j)),
        out_shape=jax.ShapeDtypeStruct((m, n), BF16),
        scratch_shapes=[pltpu.VMEM((tm, d), BF16)],
        compiler_params=_params("arbitrary", "arbitrary"),
        name="gmlp_in_proj",
    )(x2, g, sc, sh, w_all)


def _gm_gate_kernel(u_ref, v_ref, vg_ref, ws_ref, bst_ref, o_ref):
    rows, dg = v_ref.shape
    groups = dg // GM_GROUP
    v = v_ref[...].astype(F32)
    r = lax.rsqrt(jnp.mean(v * v, axis=-1, keepdims=True) + EPS)
    vn = ((v * r) * vg_ref[...]).astype(BF16)
    row = lax.broadcasted_iota(jnp.int32, (CHUNK, CHUNK), 0)
    col = lax.broadcasted_iota(jnp.int32, (CHUNK, CHUNK), 1)
    causal = col <= row
    for g in range(groups):
        w = jnp.where(causal, ws_ref[g], 0.0).astype(BF16)
        bias = bst_ref[:, g:g + 1]
        lo = g * GM_GROUP
        for c in range(rows // CHUNK):
            r0 = c * CHUNK
            sv = jnp.dot(w, vn[r0:r0 + CHUNK, lo:lo + GM_GROUP], preferred_element_type=F32) + bias
            u = u_ref[r0:r0 + CHUNK, lo:lo + GM_GROUP].astype(F32)
            o_ref[r0:r0 + CHUNK, lo:lo + GM_GROUP] = (u * sv).astype(BF16)


def _gm_gate(z, v_g, w_s, b_s_t, rows_per_batch):
    m, n2 = z.shape
    dg = n2 // 2
    groups = dg // GM_GROUP
    tr = _pick(rows_per_batch, (512, 256, 128))
    return pl.pallas_call(
        _gm_gate_kernel,
        grid=(m // tr,),
        in_specs=[
            pl.BlockSpec((tr, dg), lambda i: (i, 0)),
            pl.BlockSpec((tr, dg), lambda i: (i, 1)),
            pl.BlockSpec((1, dg), lambda i: (0, 0)),
            pl.BlockSpec((groups, CHUNK, CHUNK), lambda i: (0, 0, 0)),
            pl.BlockSpec((CHUNK, groups), lambda i: (0, 0)),
        ],
        out_specs=pl.BlockSpec((tr, dg), lambda i: (i, 0)),
        out_shape=jax.ShapeDtypeStruct((m, dg), BF16),
        compiler_params=_params("arbitrary"),
        name="gmlp_spatial_gate",
    )(z, z, v_g, w_s, b_s_t)


def _ffn_in_kernel(x_ref, g_ref, sc_ref, sh_ref, wg_ref, wu_ref, cwg_ref, cwu_ref, cbg_ref, cbu_ref,
                   o_ref, h_ref, carry_g_ref, carry_u_ref, *, tiles_per_batch):
    i = pl.program_id(0)
    j = pl.program_id(1)
    tm = x_ref.shape[0]
    halo = SUBLANES
    rc = min(tm, FFN_ROW_CHUNK)

    @pl.when((i % tiles_per_batch) == 0)
    def _():
        carry_g_ref[j] = jnp.zeros(carry_g_ref.shape[1:], F32)
        carry_u_ref[j] = jnp.zeros(carry_u_ref.shape[1:], F32)

    tn = wg_ref.shape[1]
    wc = min(tn, FFN_EPILOGUE_COLS)
    sub = lax.broadcasted_iota(jnp.int32, (halo, wc), 0)

    def shifted(acc, tail, k):
        moved = pltpu.roll(acc, k, 0)
        head = jnp.where(sub < k, pltpu.roll(tail, k, 0), moved[0:halo, :])
        return jnp.concatenate([head, moved[halo:, :]], axis=0)

    def conv(acc, tail, cw_ref, cb_ref, cols):
        acc, tail = acc[:, cols], tail[:, cols]
        return (cw_ref[0:1, cols] * shifted(acc, tail, 2) + cw_ref[1:2, cols] * shifted(acc, tail, 1)
                + cw_ref[2:3, cols] * acc + cb_ref[:, cols])

    sizes = [rc] * (tm // rc)
    if rc // 2 >= 128 and rc % 16 == 0:
        sizes = sizes[:-1] + [rc // 2, rc // 2]

    def sweep(normalize):
        tail_g = carry_g_ref[j]
        tail_u = carry_u_ref[j]
        r0 = 0
        for n_rows in sizes:
            rows = slice(r0, r0 + n_rows)
            if normalize:
                _norm_modulate(x_ref, g_ref, sc_ref, sh_ref, h_ref, rows)
            h = h_ref[rows, :]
            acc_g = jnp.dot(h, wg_ref[...], preferred_element_type=F32)
            acc_u = jnp.dot(h, wu_ref[...], preferred_element_type=F32)
            for c0 in range(0, tn, wc):
                cols = slice(c0, c0 + wc)
                gate = conv(acc_g, tail_g, cwg_ref, cbg_ref, cols)
                up = conv(acc_u, tail_u, cwu_ref, cbu_ref, cols)
                o_ref[rows, cols] = (_silu(gate) * up).astype(BF16)
            tail_g = acc_g[n_rows - halo:n_rows, :]
            tail_u = acc_u[n_rows - halo:n_rows, :]
            r0 += n_rows
        carry_g_ref[j] = tail_g
        carry_u_ref[j] = tail_u

    @pl.when(j == 0)
    def _():
        sweep(True)

    @pl.when(j != 0)
    def _():
        sweep(False)


def _ffn_in(x2, g, sc, sh, wg_all, wu_all, layer, cwg, cwu, cbg, cbu, rows_per_batch):
    m, d = x2.shape
    fp = wg_all.shape[2]
    tm = _pick(rows_per_batch, (1024, 512, 256, 128))
    tn = FFN_COL_TILE
    nj = fp // tn
    kern = functools.partial(_ffn_in_kernel, tiles_per_batch=rows_per_batch // tm)
    col = lambda i, j: (0, j)
    wcol = lambda i, j: (layer, 0, j)
    return pl.pallas_call(
        kern,
        grid=(m // tm, nj),
        in_specs=_row_specs(tm, d, rows_per_batch) + [
            pl.BlockSpec((None, d, tn), wcol),
            pl.BlockSpec((None, d, tn), wcol),
            pl.BlockSpec((CONV_W, tn), col),
            pl.BlockSpec((CONV_W, tn), col),
            pl.BlockSpec((1, tn), col),
            pl.BlockSpec((1, tn), col),
        ],
        out_specs=pl.BlockSpec((tm, tn), lambda i, j: (i, j)),
        out_shape=jax.ShapeDtypeStruct((m, fp), BF16),
        scratch_shapes=[
            pltpu.VMEM((tm, d), BF16),
            pltpu.VMEM((nj, SUBLANES, tn), F32),
            pltpu.VMEM((nj, SUBLANES, tn), F32),
        ],
        compiler_params=_params("arbitrary", "arbitrary"),
        name="ffn_in_conv_gate",
    )(x2, g, sc, sh, wg_all, wu_all, cwg, cwu, cbg, cbu)


def _out_proj_kernel(a_ref, w_ref, x_ref, gate_ref, o_ref):
    y = jnp.dot(a_ref[...], w_ref[...], preferred_element_type=F32)
    o_ref[...] = x_ref[...] + gate_ref[0] * y


def _out_proj_residual(a, w_all, layer, x2, gate, rows_per_batch):
    m = a.shape[0]
    k, n = w_all.shape[1:]
    assert k % LANES == 0 and k <= a.shape[1]
    tm = _pick(rows_per_batch, (1024, 512, 256, 128))
    tn = _pick(n, (1024, 512, 256, 128))
    if k > 2 * n:
        tn = _pick(n, (512, 256, 128))
    elif 2 * k * n * 2 <= RESIDENT_WEIGHT_BYTES:
        tn = n
        tm = _pick(rows_per_batch, (512, 256, 128))
    tiles_per_batch = rows_per_batch // tm
    return pl.pallas_call(
        _out_proj_kernel,
        grid=(m // tm, n // tn),
        in_specs=[
            pl.BlockSpec((tm, k), lambda i, j: (i, 0)),
            pl.BlockSpec((None, k, tn), lambda i, j: (layer, 0, j)),
            pl.BlockSpec((tm, tn), lambda i, j: (i, j)),
            pl.BlockSpec((1, 1, tn), lambda i, j: (i // tiles_per_batch, 0, j)),
        ],
        out_specs=pl.BlockSpec((tm, tn), lambda i, j: (i, j)),
        out_shape=jax.ShapeDtypeStruct((m, n), F32),
        compiler_params=_params("arbitrary", "arbitrary"),
        name="out_proj_residual",
    )(a, w_all, x2, gate)


def _final_norm_kernel(x_ref, g_ref, o_ref):
    x = x_ref[...]
    r = lax.rsqrt(jnp.mean(x * x, axis=-1, keepdims=True) + EPS)
    o_ref[...] = (x * r) * g_ref[...]


def _final_norm(x2, g):
    m, d = x2.shape
    tm = _pick(m, (1024, 512, 256, 128))
    return pl.pallas_call(
        _final_norm_kernel,
        grid=(m // tm,),
        in_specs=[pl.BlockSpec((tm, d), lambda i: (i, 0)), pl.BlockSpec((1, d), lambda i: (0, 0))],
        out_specs=pl.BlockSpec((tm, d), lambda i: (i, 0)),
        out_shape=jax.ShapeDtypeStruct((m, d), F32),
        compiler_params=_params("arbitrary"),
        name="final_rms_norm",
    )(x2, g)


def _pad_cols(a, n):
    return jnp.pad(a, ((0, 0), (0, n - a.shape[1])))


def kernel(x, c, mod_w, mod_b, mix_norm_g, ffn_norm_g, attn_w_in, attn_b_f, attn_w_o, gm_w_in, gm_v_g, gm_w_s,
           gm_b_s, gm_w_o, ffn_w_in, ffn_conv_w, ffn_conv_b, ffn_w_out, final_g):
    batch, seq, d = x.shape
    depth = mod_w.shape[0]
    heads = d // HEAD_DIM
    d_ff = ffn_w_in.shape[-1] // 2
    fp = -(-d_ff // FFN_COL_TILE) * FFN_COL_TILE
    m = batch * seq
    assert d % LANES == 0 and seq % CHUNK == 0 and heads <= LANES
    t_attn = _pick(seq, (512, 256, 128))

    x2 = x.reshape(m, d)
    mod = _modulation(c, mod_w, mod_b).reshape(depth, batch, 6, 1, d)

    pad_ff = ((0, 0), (0, 0), (0, fp - d_ff))
    attn_w_in_b = attn_w_in.astype(BF16)
    attn_w_o_b = attn_w_o.astype(BF16)
    gm_w_in_b = gm_w_in.astype(BF16)
    gm_w_o_b = gm_w_o.astype(BF16)
    ffn_wg_b = jnp.pad(ffn_w_in[:, :, :d_ff], pad_ff).astype(BF16)
    ffn_wu_b = jnp.pad(ffn_w_in[:, :, d_ff:], pad_ff).astype(BF16)
    ffn_w_out_b = ffn_w_out.astype(BF16)

    for i in range(depth):
        sh1, sc1, g1, sh2, sc2, g2 = (mod[i, :, k] for k in range(6))
        j = i // 2
        gm = mix_norm_g[i].reshape(1, d)
        if i % 2 == 0:
            w_f = _pad_cols(attn_w_in[j][:, 3 * d:], LANES).astype(BF16)
            b_f = _pad_cols(attn_b_f[j].reshape(1, heads), LANES)
            qkv, flog = _attn_in_proj(x2, gm, sc1, sh1, attn_w_in_b, j, w_f, b_f, seq)
            f, ft = _forget_cumsum(flog, batch, seq)
            ft4 = ft[:, :heads, :].reshape(batch, heads, seq // t_attn, t_attn)
            y = _fox_attention(qkv, f, ft4, batch, seq, d)
            x2 = _out_proj_residual(y, attn_w_o_b, j, x2, g1, seq)
        else:
            z = _gm_in_proj(x2, gm, sc1, sh1, gm_w_in_b, j, seq)
            y = _gm_gate(z, gm_v_g[j].reshape(1, -1), gm_w_s[j], gm_b_s[j].T, seq)
            x2 = _out_proj_residual(y, gm_w_o_b, j, x2, g1, seq)

        cw = ffn_conv_w[i]
        cb = ffn_conv_b[i].reshape(1, 2 * d_ff)
        a = _ffn_in(x2, ffn_norm_g[i].reshape(1, d), sc2, sh2, ffn_wg_b, ffn_wu_b, i,
                    _pad_cols(cw[:, :d_ff], fp), _pad_cols(cw[:, d_ff:], fp),
                    _pad_cols(cb[:, :d_ff], fp), _pad_cols(cb[:, d_ff:], fp), seq)
        x2 = _out_proj_residual(a, ffn_w_out_b, i, x2, g2, seq)

    return _final_norm(x2, final_g.reshape(1, d)).reshape(batch, seq, d)
```

```python
import functools

import jax
import jax.numpy as jnp
from jax import lax
from jax.experimental import pallas as pl
from jax.experimental.pallas import tpu as pltpu

HEAD_DIM = 128
CHUNK = 128
GM_GROUP = 128
CONV_W = 3
EPS = 1e-6
LOG2E = 1.4426950408889634
LANES = 128
SUBLANES = 8
VMEM_LIMIT_BYTES = 56 * 1024 * 1024
RESIDENT_WEIGHT_BYTES = 16 * 1024 * 1024
FAST_PATH_LIMIT = 1e30
FFN_ROW_CHUNK = 128
FFN_COL_TILE = 512
NORM_ROW_CHUNK = 256

F32 = jnp.float32
BF16 = jnp.bfloat16


def _pick(n, prefs):
    for p in prefs:
        if n % p == 0:
            return p
    return n


def _params(*sem):
    return pltpu.CompilerParams(dimension_semantics=sem, vmem_limit_bytes=VMEM_LIMIT_BYTES)


def _silu(x):
    return x / (1.0 + jnp.exp(-x))


def _mod_kernel(c_ref, w_ref, b_ref, o_ref):
    ca = _silu(c_ref[...]).astype(BF16)
    o_ref[0] = jnp.dot(ca, w_ref[0].astype(BF16), preferred_element_type=F32) + b_ref[0]


def _modulation(c, mod_w, mod_b):
    depth, d, n = mod_w.shape
    b = c.shape[0]
    tn = _pick(n, (1024, 512, 256, 128))
    return pl.pallas_call(
        _mod_kernel,
        grid=(depth, n // tn),
        in_specs=[
            pl.BlockSpec((b, d), lambda l, j: (0, 0)),
            pl.BlockSpec((1, d, tn), lambda l, j: (l, 0, j)),
            pl.BlockSpec((1, 1, tn), lambda l, j: (l, 0, j)),
        ],
        out_specs=pl.BlockSpec((1, b, tn), lambda l, j: (l, 0, j)),
        out_shape=jax.ShapeDtypeStruct((depth, b, n), F32),
        compiler_params=_params("arbitrary", "arbitrary"),
        name="adaln_modulation",
    )(c, mod_w, mod_b.reshape(depth, 1, n))


def _norm_modulate(x_ref, g_ref, sc_ref, sh_ref, h_ref, rows=slice(None)):
    x = x_ref[rows, :]
    r = lax.rsqrt(jnp.mean(x * x, axis=-1, keepdims=True) + EPS)
    a = g_ref[...] * (1.0 + sc_ref[0])
    h_ref[rows, :] = ((x * r) * a + sh_ref[0]).astype(BF16)


def _row_chunks(tm):
    rc = min(tm, NORM_ROW_CHUNK)
    return [slice(r0, r0 + rc) for r0 in range(0, tm, rc)]


def _row_specs(tm, d, rows_per_batch):
    tiles_per_batch = rows_per_batch // tm
    return [
        pl.BlockSpec((tm, d), lambda i, j: (i, 0)),
        pl.BlockSpec((1, d), lambda i, j: (0, 0)),
        pl.BlockSpec((1, 1, d), lambda i, j: (i // tiles_per_batch, 0, 0)),
        pl.BlockSpec((1, 1, d), lambda i, j: (i // tiles_per_batch, 0, 0)),
    ]


def _attn_in_kernel(x_ref, g_ref, sc_ref, sh_ref, w_ref, wf_ref, bf_ref, qkv_ref, fl_ref, h_ref,
                    *, q_tiles, q_scale):
    j = pl.program_id(1)
    scale = jnp.where(j < q_tiles, q_scale, 1.0)

    @pl.when(j == 0)
    def _():
        for rows in _row_chunks(x_ref.shape[0]):
            _norm_modulate(x_ref, g_ref, sc_ref, sh_ref, h_ref, rows)
            h = h_ref[rows, :]
            fl_ref[rows, :] = jnp.dot(h, wf_ref[...], preferred_element_type=F32) + bf_ref[...]
            qkv_ref[rows, :] = (jnp.dot(h, w_ref[...], preferred_element_type=F32) * scale).astype(BF16)

    @pl.when(j != 0)
    def _():
        acc = jnp.dot(h_ref[...], w_ref[...], preferred_element_type=F32)
        qkv_ref[...] = (acc * scale).astype(BF16)


def _attn_in_proj(x2, g, sc, sh, w_in_all, layer, w_f, b_f, rows_per_batch):
    m, d = x2.shape
    n = 3 * d
    tm = _pick(rows_per_batch, (1024, 512, 256, 128))
    tn = _pick(d, (2048, 1024, 512, 256, 128))
    kern = functools.partial(_attn_in_kernel, q_tiles=d // tn, q_scale=HEAD_DIM ** -0.5 * LOG2E)
    return pl.pallas_call(
        kern,
        grid=(m // tm, n // tn),
        in_specs=_row_specs(tm, d, rows_per_batch) + [
            pl.BlockSpec((None, d, tn), lambda i, j: (layer, 0, j)),
            pl.BlockSpec((d, LANES), lambda i, j: (0, 0)),
            pl.BlockSpec((1, LANES), lambda i, j: (0, 0)),
        ],
        out_specs=[
            pl.BlockSpec((tm, tn), lambda i, j: (i, j)),
            pl.BlockSpec((tm, LANES), lambda i, j: (i, 0)),
        ],
        out_shape=[
            jax.ShapeDtypeStruct((m, n), BF16),
            jax.ShapeDtypeStruct((m, LANES), F32),
        ],
        scratch_shapes=[pltpu.VMEM((tm, d), BF16)],
        compiler_params=_params("arbitrary", "arbitrary"),
        name="attn_in_proj",
    )(x2, g, sc, sh, w_in_all, w_f, b_f)


def _split3(x):
    hi = x.astype(BF16)
    r1 = x - hi.astype(F32)
    mid = r1.astype(BF16)
    lo = (r1 - mid.astype(F32)).astype(BF16)
    return hi, mid, lo


def _forget_cumsum_kernel(fl_ref, f_ref, ft_ref, carry_ref, *, sub):
    @pl.when(pl.program_id(1) == 0)
    def _():
        carry_ref[...] = jnp.zeros_like(carry_ref)

    tc = fl_ref.shape[0]
    row = lax.broadcasted_iota(jnp.int32, (sub, sub), 0)
    col = lax.broadcasted_iota(jnp.int32, (sub, sub), 1)
    tril = jnp.where(row >= col, 1.0, 0.0).astype(BF16)
    carry = carry_ref[0:1, :]
    for r in range(tc // sub):
        z = fl_ref[r * sub:(r + 1) * sub, :]
        lf = jnp.minimum(z, 0.0) - jnp.log1p(jnp.exp(-jnp.abs(z)))
        hi, mid, lo = _split3(lf)
        c = (jnp.dot(tril, lo, preferred_element_type=F32)
             + jnp.dot(tril, mid, preferred_element_type=F32)
             + jnp.dot(tril, hi, preferred_element_type=F32)) + carry
        c2 = c * LOG2E
        f_ref[r * sub:(r + 1) * sub, :] = c2
        ft_ref[0, :, r * sub:(r + 1) * sub] = c2.T
        carry = c[sub - 1:sub, :]
    carry_ref[0:1, :] = carry


def _forget_cumsum(flog, batch, seq):
    tc = _pick(seq, (1024, 512, 256, 128))
    sub = _pick(tc, (256, 128))
    nt = seq // tc
    return pl.pallas_call(
        functools.partial(_forget_cumsum_kernel, sub=sub),
        grid=(batch, nt),
        in_specs=[pl.BlockSpec((tc, LANES), lambda b, t: (b * nt + t, 0))],
        out_specs=[
            pl.BlockSpec((tc, LANES), lambda b, t: (b * nt + t, 0)),
            pl.BlockSpec((1, LANES, tc), lambda b, t: (b, 0, t)),
        ],
        out_shape=[
            jax.ShapeDtypeStruct((batch * seq, LANES), F32),
            jax.ShapeDtypeStruct((batch, LANES, seq), F32),
        ],
        scratch_shapes=[pltpu.VMEM((SUBLANES, LANES), F32)],
        compiler_params=_params("arbitrary", "arbitrary"),
        name="forget_cumsum",
    )(flog)


def _fox_attn_kernel(q_ref, k_ref, v_ref, f_ref, ft_ref, o_ref, vt_ref, fkb_ref, vmax_ref, m_ref, l_ref, acc_ref,
                     *, t, hg):
    qi = pl.program_id(2)
    nq = vt_ref.shape[1]
    heads = [pl.program_id(1) * hg + g for g in range(hg)]

    @pl.when(qi == 0)
    def _():
        lane = lax.broadcasted_iota(jnp.int32, (t, LANES), 1)
        for g in range(hg):
            v_abs = jnp.abs(v_ref[:, g * HEAD_DIM:(g + 1) * HEAD_DIM].astype(F32))
            vmax_ref[g] = jnp.max(v_abs, axis=0, keepdims=True)
            for c in range(nq):
                rows = slice(c * t, (c + 1) * t)
                vt_ref[g, c] = v_ref[rows, g * HEAD_DIM:(g + 1) * HEAD_DIM].T
                fk = jnp.sum(jnp.where(lane == heads[g], f_ref[rows, :], 0.0), axis=-1, keepdims=True)
                fkb_ref[g, rows, :] = jnp.broadcast_to(fk, (t, LANES))

    qs = [q_ref[:, g * HEAD_DIM:(g + 1) * HEAD_DIM] for g in range(hg)]
    fqs = [ft_ref[0, heads[g], pl.ds(qi, 1), :] for g in range(hg)]

    def scores(g, j):
        start = pl.multiple_of(j * t, t)
        kj = k_ref[pl.ds(start, t), g * HEAD_DIM:(g + 1) * HEAD_DIM]
        s = lax.dot_general(kj, qs[g], (((1,), (1,)), ((), ())), preferred_element_type=F32)
        return s - jnp.tile(fkb_ref[g, pl.ds(start, t), :], (1, t // LANES))

    def update(g, j, s, m_prev, l_prev, acc_prev):
        m_new = jnp.maximum(m_prev, jnp.max(s, axis=0, keepdims=True) + fqs[g])
        alpha = jnp.exp2(m_prev - m_new)
        p = jnp.exp2(s - (m_new - fqs[g]))
        l_new = alpha * l_prev + jnp.sum(p, axis=0, keepdims=True)
        pv = jnp.dot(vt_ref[g, j], p.astype(BF16), preferred_element_type=F32)
        return m_new, l_new, alpha * acc_prev + pv

    def load_state(g):
        return m_ref[g], l_ref[g], acc_ref[g]

    def store_state(g, state):
        m_ref[g], l_ref[g], acc_ref[g] = state

    def exact_body(it, _):
        j = qi - 1 - it
        for g in range(hg):
            store_state(g, update(g, j, scores(g, j), *load_state(g)))
        return 0

    def fast_body(it, _):
        j = qi - 1 - it
        s_next = scores(0, j)
        p_prev = None
        for g in range(hg):
            s_cur = s_next
            if g + 1 < hg:
                s_next = scores(g + 1, j)
            p = jnp.exp2(s_cur - (m_ref[g] - fqs[g]))
            l_ref[g] += jnp.sum(p, axis=0, keepdims=True)
            if p_prev is not None:
                acc_ref[g - 1] += jnp.dot(vt_ref[g - 1, j], p_prev, preferred_element_type=F32)
            p_prev = p.astype(BF16)
        acc_ref[hg - 1] += jnp.dot(vt_ref[hg - 1, j], p_prev, preferred_element_type=F32)
        return 0

    half = t // 2
    causal = (lax.broadcasted_iota(jnp.int32, (half, half), 0) <= lax.broadcasted_iota(jnp.int32, (half, half), 1))

    def diag_scores(g):
        start = pl.multiple_of(qi * t, t)
        cols = slice(g * HEAD_DIM, (g + 1) * HEAD_DIM)
        nt = (((1,), (1,)), ((), ()))
        sa = lax.dot_general(k_ref[pl.ds(start, half), cols], qs[g], nt, preferred_element_type=F32)
        sa = sa - jnp.tile(fkb_ref[g, pl.ds(start, half), :], (1, t // LANES))
        sb = lax.dot_general(k_ref[pl.ds(start + half, half), cols], qs[g][half:, :], nt,
                             preferred_element_type=F32)
        sb = sb - jnp.tile(fkb_ref[g, pl.ds(start + half, half), :], (1, half // LANES))
        sa = jnp.concatenate([jnp.where(causal, sa[:, :half], -jnp.inf), sa[:, half:]], axis=1)
        return sa, jnp.where(causal, sb, -jnp.inf)

    def late(x_all, x_late, op):
        return jnp.concatenate([x_all[:, :half], op(x_all[:, half:], x_late)], axis=1)

    def diag_update(g, sa, sb):
        m_new = late(jnp.max(sa, axis=0, keepdims=True), jnp.max(sb, axis=0, keepdims=True), jnp.maximum) + fqs[g]
        shift = m_new - fqs[g]
        pa = jnp.exp2(sa - shift)
        pb = jnp.exp2(sb - shift[:, half:])
        l_new = late(jnp.sum(pa, axis=0, keepdims=True), jnp.sum(pb, axis=0, keepdims=True), jnp.add)
        vt = vt_ref[g, qi]
        acc_a = jnp.dot(vt[:, :half], pa.astype(BF16), preferred_element_type=F32)
        acc_b = jnp.dot(vt[:, half:], pb.astype(BF16), preferred_element_type=F32)
        return m_new, l_new, late(acc_a, acc_b, jnp.add)

    def diag_block():
        s_diag = [diag_scores(g) for g in range(hg)]
        for g in range(hg):
            store_state(g, diag_update(g, *s_diag[g]))

    diag_block()
    lax.fori_loop(0, qi, fast_body, 0)

    risk = jnp.zeros((1, t), F32)
    for g in range(hg):
        bound = l_ref[g] * jnp.maximum(jnp.max(vmax_ref[g], axis=-1, keepdims=True), 1.0)
        risk = jnp.maximum(risk, jnp.where(bound < FAST_PATH_LIMIT, 0.0, 1.0))

    @pl.when(jnp.max(risk) > 0.0)
    def _():
        diag_block()
        lax.fori_loop(0, qi, exact_body, 0)

    for g in range(hg):
        inv_l = 1.0 / l_ref[g]
        o_ref[:, g * HEAD_DIM:(g + 1) * HEAD_DIM] = (acc_ref[g] * inv_l).T.astype(BF16)


def _fox_attention(qkv, f, ft4, batch, seq, d):
    heads = d // HEAD_DIM
    hg = _pick(heads, (4, 2, 1))
    wg = hg * HEAD_DIM
    ng = heads // hg
    t = ft4.shape[-1]
    assert t % (2 * LANES) == 0, "the diagonal block is processed in lane-aligned halves"
    nq = seq // t
    return pl.pallas_call(
        functools.partial(_fox_attn_kernel, t=t, hg=hg),
        grid=(batch, ng, nq),
        in_specs=[
            pl.BlockSpec((t, wg), lambda b, h, i: (b * nq + i, h)),
            pl.BlockSpec((seq, wg), lambda b, h, i: (b, ng + h)),
            pl.BlockSpec((seq, wg), lambda b, h, i: (b, 2 * ng + h)),
            pl.BlockSpec((seq, LANES), lambda b, h, i: (b, 0)),
            pl.BlockSpec((1, heads, nq, t), lambda b, h, i: (b, 0, 0, 0)),
        ],
        out_specs=pl.BlockSpec((t, wg), lambda b, h, i: (b * nq + i, h)),
        out_shape=jax.ShapeDtypeStruct((batch * seq, d), BF16),
        scratch_shapes=[
            pltpu.VMEM((hg, nq, HEAD_DIM, t), BF16),
            pltpu.VMEM((hg, seq, LANES), F32),
            pltpu.VMEM((hg, 1, HEAD_DIM), F32),
            pltpu.VMEM((hg, 1, t), F32),
            pltpu.VMEM((hg, 1, t), F32),
            pltpu.VMEM((hg, HEAD_DIM, t), F32),
        ],
        compiler_params=_params("arbitrary", "arbitrary", "arbitrary"),
        name="fox_attention",
    )(qkv, qkv, qkv, f, ft4)


def _gelu_tanh(x):
    c = 0.7978845608028654
    return 0.5 * x * (1.0 + jnp.tanh(c * (x + 0.044715 * (x * x * x))))


def _gm_in_kernel(x_ref, g_ref, sc_ref, sh_ref, w_ref, z_ref, h_ref):
    j = pl.program_id(1)

    @pl.when(j == 0)
    def _():
        for rows in _row_chunks(x_ref.shape[0]):
            _norm_modulate(x_ref, g_ref, sc_ref, sh_ref, h_ref, rows)
            acc = jnp.dot(h_ref[rows, :], w_ref[...], preferred_element_type=F32)
            z_ref[rows, :] = _gelu_tanh(acc).astype(BF16)

    @pl.when(j != 0)
    def _():
        acc = jnp.dot(h_ref[...], w_ref[...], preferred_element_type=F32)
        z_ref[...] = _gelu_tanh(acc).astype(BF16)


def _gm_in_proj(x2, g, sc, sh, w_all, layer, rows_per_batch):
    m, d = x2.shape
    n = w_all.shape[2]
    tm = _pick(rows_per_batch, (1024, 512, 256, 128))
    tn = _pick(n, (2048, 1024, 512, 256, 128))
    return pl.pallas_call(
        _gm_in_kernel,
        grid=(m // tm, n // tn),
        in_specs=_row_specs(tm, d, rows_per_batch) + [
            pl.BlockSpec((None, d, tn), lambda i, j: (layer, 0, j))],
        out_specs=pl.BlockSpec((tm, tn), lambda i, j: (i, j)),
        out_shape=jax.ShapeDtypeStruct((m, n), BF16),
        scratch_shapes=[pltpu.VMEM((tm, d), BF16)],
        compiler_params=_params("arbitrary", "arbitrary"),
        name="gmlp_in_proj",
    )(x2, g, sc, sh, w_all)


def _gm_gate_kernel(u_ref, v_ref, vg_ref, ws_ref, bst_ref, o_ref):
    rows, dg = v_ref.shape
    groups = dg // GM_GROUP
    v = v_ref[...].astype(F32)
    r = lax.rsqrt(jnp.mean(v * v, axis=-1, keepdims=True) + EPS)
    vn = ((v * r) * vg_ref[...]).astype(BF16)
    row = lax.broadcasted_iota(jnp.int32, (CHUNK, CHUNK), 0)
    col = lax.broadcasted_iota(jnp.int32, (CHUNK, CHUNK), 1)
    causal = col <= row
    for g in range(groups):
        w = jnp.where(causal, ws_ref[g], 0.0).astype(BF16)
        bias = bst_ref[:, g:g + 1]
        lo = g * GM_GROUP
        for c in range(rows // CHUNK):
            r0 = c * CHUNK
            sv = jnp.dot(w, vn[r0:r0 + CHUNK, lo:lo + GM_GROUP], preferred_element_type=F32) + bias
            u = u_ref[r0:r0 + CHUNK, lo:lo + GM_GROUP].astype(F32)
            o_ref[r0:r0 + CHUNK, lo:lo + GM_GROUP] = (u * sv).astype(BF16)


def _gm_gate(z, v_g, w_s, b_s_t, rows_per_batch):
    m, n2 = z.shape
    dg = n2 // 2
    groups = dg // GM_GROUP
    tr = _pick(rows_per_batch, (512, 256, 128))
    return pl.pallas_call(
        _gm_gate_kernel,
        grid=(m // tr,),
        in_specs=[
            pl.BlockSpec((tr, dg), lambda i: (i, 0)),
            pl.BlockSpec((tr, dg), lambda i: (i, 1)),
            pl.BlockSpec((1, dg), lambda i: (0, 0)),
            pl.BlockSpec((groups, CHUNK, CHUNK), lambda i: (0, 0, 0)),
            pl.BlockSpec((CHUNK, groups), lambda i: (0, 0)),
        ],
        out_specs=pl.BlockSpec((tr, dg), lambda i: (i, 0)),
        out_shape=jax.ShapeDtypeStruct((m, dg), BF16),
        compiler_params=_params("arbitrary"),
        name="gmlp_spatial_gate",
    )(z, z, v_g, w_s, b_s_t)


def _ffn_in_kernel(x_ref, g_ref, sc_ref, sh_ref, wg_ref, wu_ref, cwg_ref, cwu_ref, cbg_ref, cbu_ref,
                   o_ref, h_ref, carry_g_ref, carry_u_ref, *, tiles_per_batch):
    i = pl.program_id(0)
    j = pl.program_id(1)
    tm = x_ref.shape[0]
    halo = SUBLANES
    rc = min(tm, FFN_ROW_CHUNK)

    @pl.when((i % tiles_per_batch) == 0)
    def _():
        carry_g_ref[j] = jnp.zeros(carry_g_ref.shape[1:], F32)
        carry_u_ref[j] = jnp.zeros(carry_u_ref.shape[1:], F32)

    sub = lax.broadcasted_iota(jnp.int32, (halo, wg_ref.shape[1]), 0)

    def shifted(acc, tail, k):
        moved = pltpu.roll(acc, k, 0)
        head = jnp.where(sub < k, pltpu.roll(tail, k, 0), moved[0:halo, :])
        return jnp.concatenate([head, moved[halo:, :]], axis=0)

    def conv(acc, tail, cw_ref, cb_ref):
        return (cw_ref[0:1, :] * shifted(acc, tail, 2) + cw_ref[1:2, :] * shifted(acc, tail, 1)
                + cw_ref[2:3, :] * acc + cb_ref[...])

    sizes = [rc] * (tm // rc)
    if rc // 2 >= 128 and rc % 16 == 0:
        sizes = sizes[:-1] + [rc // 2, rc // 2]

    def sweep(normalize):
        tail_g = carry_g_ref[j]
        tail_u = carry_u_ref[j]
        r0 = 0
        for n_rows in sizes:
            rows = slice(r0, r0 + n_rows)
            if normalize:
                _norm_modulate(x_ref, g_ref, sc_ref, sh_ref, h_ref, rows)
            h = h_ref[rows, :]
            acc_g = jnp.dot(h, wg_ref[...], preferred_element_type=F32)
            acc_u = jnp.dot(h, wu_ref[...], preferred_element_type=F32)
            gate = conv(acc_g, tail_g, cwg_ref, cbg_ref)
            up = conv(acc_u, tail_u, cwu_ref, cbu_ref)
            o_ref[rows, :] = (_silu(gate) * up).astype(BF16)
            tail_g = acc_g[n_rows - halo:n_rows, :]
            tail_u = acc_u[n_rows - halo:n_rows, :]
            r0 += n_rows
        carry_g_ref[j] = tail_g
        carry_u_ref[j] = tail_u

    @pl.when(j == 0)
    def _():
        sweep(True)

    @pl.when(j != 0)
    def _():
        sweep(False)


def _ffn_in(x2, g, sc, sh, wg_all, wu_all, layer, cwg, cwu, cbg, cbu, rows_per_batch):
    m, d = x2.shape
    fp = wg_all.shape[2]
    tm = _pick(rows_per_batch, (1024, 512, 256, 128))
    tn = FFN_COL_TILE
    nj = fp // tn
    kern = functools.partial(_ffn_in_kernel, tiles_per_batch=rows_per_batch // tm)
    col = lambda i, j: (0, j)
    wcol = lambda i, j: (layer, 0, j)
    return pl.pallas_call(
        kern,
        grid=(m // tm, nj),
        in_specs=_row_specs(tm, d, rows_per_batch) + [
            pl.BlockSpec((None, d, tn), wcol),
            pl.BlockSpec((None, d, tn), wcol),
            pl.BlockSpec((CONV_W, tn), col),
            pl.BlockSpec((CONV_W, tn), col),
            pl.BlockSpec((1, tn), col),
            pl.BlockSpec((1, tn), col),
        ],
        out_specs=pl.BlockSpec((tm, tn), lambda i, j: (i, j)),
        out_shape=jax.ShapeDtypeStruct((m, fp), BF16),
        scratch_shapes=[
            pltpu.VMEM((tm, d), BF16),
            pltpu.VMEM((nj, SUBLANES, tn), F32),
            pltpu.VMEM((nj, SUBLANES, tn), F32),
        ],
        compiler_params=_params("arbitrary", "arbitrary"),
        name="ffn_in_conv_gate",
    )(x2, g, sc, sh, wg_all, wu_all, cwg, cwu, cbg, cbu)


def _out_proj_kernel(a_ref, w_ref, x_ref, gate_ref, o_ref):
    y = jnp.dot(a_ref[...], w_ref[...], preferred_element_type=F32)
    o_ref[...] = x_ref[...] + gate_ref[0] * y


def _out_proj_residual(a, w_all, layer, x2, gate, rows_per_batch):
    m = a.shape[0]
    k, n = w_all.shape[1:]
    assert k % LANES == 0 and k <= a.shape[1]
    tm = _pick(rows_per_batch, (1024, 512, 256, 128))
    tn = _pick(n, (1024, 512, 256, 128))
    if k > 2 * n:
        tn = _pick(n, (512, 256, 128))
    elif 2 * k * n * 2 <= RESIDENT_WEIGHT_BYTES:
        tn = n
        tm = _pick(rows_per_batch, (512, 256, 128))
    tiles_per_batch = rows_per_batch // tm
    return pl.pallas_call(
        _out_proj_kernel,
        grid=(m // tm, n // tn),
        in_specs=[
            pl.BlockSpec((tm, k), lambda i, j: (i, 0)),
            pl.BlockSpec((None, k, tn), lambda i, j: (layer, 0, j)),
            pl.BlockSpec((tm, tn), lambda i, j: (i, j)),
            pl.BlockSpec((1, 1, tn), lambda i, j: (i // tiles_per_batch, 0, j)),
        ],
        out_specs=pl.BlockSpec((tm, tn), lambda i, j: (i, j)),
        out_shape=jax.ShapeDtypeStruct((m, n), F32),
        compiler_params=_params("arbitrary", "arbitrary"),
        name="out_proj_residual",
    )(a, w_all, x2, gate)


def _final_norm_kernel(x_ref, g_ref, o_ref):
    x = x_ref[...]
    r = lax.rsqrt(jnp.mean(x * x, axis=-1, keepdims=True) + EPS)
    o_ref[...] = (x * r) * g_ref[...]


def _final_norm(x2, g):
    m, d = x2.shape
    tm = _pick(m, (1024, 512, 256, 128))
    return pl.pallas_call(
        _final_norm_kernel,
        grid=(m // tm,),
        in_specs=[pl.BlockSpec((tm, d), lambda i: (i, 0)), pl.BlockSpec((1, d), lambda i: (0, 0))],
        out_specs=pl.BlockSpec((tm, d), lambda i: (i, 0)),
        out_shape=jax.ShapeDtypeStruct((m, d), F32),
        compiler_params=_params("arbitrary"),
        name="final_rms_norm",
    )(x2, g)


def _pad_cols(a, n):
    return jnp.pad(a, ((0, 0), (0, n - a.shape[1])))


def kernel(x, c, mod_w, mod_b, mix_norm_g, ffn_norm_g, attn_w_in, attn_b_f, attn_w_o, gm_w_in, gm_v_g, gm_w_s,
           gm_b_s, gm_w_o, ffn_w_in, ffn_conv_w, ffn_conv_b, ffn_w_out, final_g):
    batch, seq, d = x.shape
    depth = mod_w.shape[0]
    heads = d // HEAD_DIM
    d_ff = ffn_w_in.shape[-1] // 2
    fp = -(-d_ff // FFN_COL_TILE) * FFN_COL_TILE
    m = batch * seq
    assert d % LANES == 0 and seq % CHUNK == 0 and heads <= LANES
    t_attn = _pick(seq, (512, 256, 128))

    x2 = x.reshape(m, d)
    mod = _modulation(c, mod_w, mod_b).reshape(depth, batch, 6, 1, d)

    pad_ff = ((0, 0), (0, 0), (0, fp - d_ff))
    attn_w_in_b = attn_w_in.astype(BF16)
    attn_w_o_b = attn_w_o.astype(BF16)
    gm_w_in_b = gm_w_in.astype(BF16)
    gm_w_o_b = gm_w_o.astype(BF16)
    ffn_wg_b = jnp.pad(ffn_w_in[:, :, :d_ff], pad_ff).astype(BF16)
    ffn_wu_b = jnp.pad(ffn_w_in[:, :, d_ff:], pad_ff).astype(BF16)
    ffn_w_out_b = ffn_w_out.astype(BF16)

    for i in range(depth):
        sh1, sc1, g1, sh2, sc2, g2 = (mod[i, :, k] for k in range(6))
        j = i // 2
        gm = mix_norm_g[i].reshape(1, d)
        if i % 2 == 0:
            w_f = _pad_cols(attn_w_in[j][:, 3 * d:], LANES).astype(BF16)
            b_f = _pad_cols(attn_b_f[j].reshape(1, heads), LANES)
            qkv, flog = _attn_in_proj(x2, gm, sc1, sh1, attn_w_in_b, j, w_f, b_f, seq)
            f, ft = _forget_cumsum(flog, batch, seq)
            ft4 = ft[:, :heads, :].reshape(batch, heads, seq // t_attn, t_attn)
            y = _fox_attention(qkv, f, ft4, batch, seq, d)
            x2 = _out_proj_residual(y, attn_w_o_b, j, x2, g1, seq)
        else:
            z = _gm_in_proj(x2, gm, sc1, sh1, gm_w_in_b, j, seq)
            y = _gm_gate(z, gm_v_g[j].reshape(1, -1), gm_w_s[j], gm_b_s[j].T, seq)
            x2 = _out_proj_residual(y, gm_w_o_b, j, x2, g1, seq)

        cw = ffn_conv_w[i]
        cb = ffn_conv_b[i].reshape(1, 2 * d_ff)
        a = _ffn_in(x2, ffn_norm_g[i].reshape(1, d), sc2, sh2, ffn_wg_b, ffn_wu_b, i,
                    _pad_cols(cw[:, :d_ff], fp), _pad_cols(cw[:, d_ff:], fp),
                    _pad_cols(cb[:, :d_ff], fp), _pad_cols(cb[:, d_ff:], fp), seq)
        x2 = _out_proj_residual(a, ffn_w_out_b, i, x2, g2, seq)

    return _final_norm(x2, final_g.reshape(1, d)).reshape(batch, seq, d)
```
